```python
import math
import jax, jax.numpy as jnp
from jax import lax
import numpy as np

D_MODEL = 4096
BATCH = 2
SEQ = 4096
DEPTH = 2
DEC_BATCH = 4
DEC_SEQ = 2048
PAST_LEN = 128

HEAD_DIM = 128
GRID_W = 64
N_MEM = 256
Q_BLOCK = 128
NORM_EPS = 1e-6
D_FF = 11008
A_Q_HEADS = 12
A_KV_HEADS = 4
ROPE_BASE = 10000.0
B_GROUPS = 20
B_GROUP_DIM = 128
C_PATTERNS = ((128, 1), (512, 4), (2048, 16))
C_HEADS_PER_PATTERN = 4
C_HEADS = C_HEADS_PER_PATTERN * len(C_PATTERNS)
ALIBI_MAX_EXP = 8.0
X_HEADS = 4
N_BRANCH = 3

A_Q_W = A_Q_HEADS * HEAD_DIM
A_KV_W = A_KV_HEADS * HEAD_DIM
B_W = B_GROUPS * B_GROUP_DIM
C_W = C_HEADS * HEAD_DIM
C_OUT_W = C_HEADS_PER_PATTERN * HEAD_DIM
X_W = X_HEADS * HEAD_DIM
MIX_IN_W = A_Q_W + 2 * A_KV_W + B_W + 3 * C_W
MIX_SPLITS = (A_Q_W, A_Q_W + A_KV_W, A_Q_W + 2 * A_KV_W, A_Q_W + 2 * A_KV_W + B_W,
              A_Q_W + 2 * A_KV_W + B_W + C_W, A_Q_W + 2 * A_KV_W + B_W + 2 * C_W)

kernel_name = "hybrid_gated_encoder_trunk"


def rms_norm(x, g):
    xf = x.astype(jnp.float32)
    y = xf * lax.rsqrt(jnp.mean(xf * xf, axis=-1, keepdims=True) + NORM_EPS)
    return (y * g.astype(jnp.float32)).astype(x.dtype)


def swiglu(x, w_in, w_out):
    gate, up = jnp.split(x @ w_in, 2, axis=-1)
    return (jax.nn.silu(gate) * up) @ w_out


def alibi_slopes(n):
    return 2.0 ** (-ALIBI_MAX_EXP * jnp.arange(1, n + 1, dtype=jnp.float32) / n)


def axial_angles(seq_len):
    rows = seq_len // GRID_W
    row = jnp.repeat(jnp.arange(rows, dtype=jnp.float32), GRID_W)
    col = jnp.tile(jnp.arange(GRID_W, dtype=jnp.float32), rows)
    n_pairs = HEAD_DIM // 4
    inv_freq = ROPE_BASE ** (-jnp.arange(n_pairs, dtype=jnp.float32) / n_pairs)
    ang = jnp.concatenate([row[:, None] * inv_freq[None, :], col[:, None] * inv_freq[None, :]], axis=-1)
    return jnp.cos(ang), jnp.sin(ang)


def apply_axial_rope(x, cos, sin):
    b, s, h, d = x.shape
    xf = x.astype(jnp.float32).reshape(b, s, h, d // 2, 2)
    x1, x2 = xf[..., 0], xf[..., 1]
    c, sn = cos[None, :, None, :], sin[None, :, None, :]
    out = jnp.stack([x1 * c - x2 * sn, x1 * sn + x2 * c], axis=-1)
    return out.reshape(b, s, h, d).astype(x.dtype)


def axial_gqa(a_q, a_k, a_v, q_gain, k_gain):
    b, s, _ = a_q.shape
    groups = A_Q_HEADS // A_KV_HEADS
    cos, sin = axial_angles(s)
    q = apply_axial_rope(rms_norm(a_q.reshape(b, s, A_Q_HEADS, HEAD_DIM), q_gain), cos, sin)
    k = apply_axial_rope(rms_norm(a_k.reshape(b, s, A_KV_HEADS, HEAD_DIM), k_gain), cos, sin)
    v = a_v.reshape(b, s, A_KV_HEADS, HEAD_DIM)
    q = q.reshape(b, s // Q_BLOCK, Q_BLOCK, A_KV_HEADS, groups, HEAD_DIM)
    q = jnp.moveaxis(q, 1, 0)
    scale = HEAD_DIM ** -0.5

    def block(qb):
        sc = jnp.einsum('bqhgd,bkhd->bhgqk', qb, k, preferred_element_type=jnp.float32) * scale
        p = jax.nn.softmax(sc, axis=-1).astype(v.dtype)
        return jnp.einsum('bhgqk,bkhd->bqhgd', p, v)

    o = lax.map(block, q)
    return jnp.moveaxis(o, 0, 1).reshape(b, s, A_Q_W)


def fourier_mix(b_in):
    b, s, _ = b_in.shape
    z = b_in.reshape(b, s, B_GROUPS, B_GROUP_DIM).astype(jnp.float32)
    f = jnp.fft.fft2(z, axes=(1, 3), norm="ortho").real
    return f.reshape(b, s, B_W).astype(b_in.dtype)


def strided_band_attention(q, k, v, slopes, half, dil):
    b, s, h, d = q.shape
    sub_len = s // dil
    qb_len = math.gcd(sub_len, Q_BLOCK)
    n_blocks = sub_len // qb_len
    span = qb_len + 2 * half
    ks = k.reshape(b, sub_len, dil, h, d)
    vs = v.reshape(b, sub_len, dil, h, d)
    pad = ((0, 0), (half, half), (0, 0), (0, 0), (0, 0))
    kpad, vpad = jnp.pad(ks, pad), jnp.pad(vs, pad)
    idx = jnp.arange(n_blocks)[:, None] * qb_len + jnp.arange(span)[None, :]
    kb = kpad[:, idx]
    vb = vpad[:, idx]
    qb = q.reshape(b, n_blocks, qb_len, dil, h, d)
    rel = jnp.arange(span)[None, :] - half - jnp.arange(qb_len)[:, None]
    key_pos = idx - half
    valid = (jnp.abs(rel)[None] <= half) & (key_pos[:, None, :] >= 0) & (key_pos[:, None, :] < sub_len)
    bias = -slopes[:, None, None] * (jnp.abs(rel) * dil).astype(jnp.float32)[None]
    sc = jnp.einsum('bnqrhd,bnkrhd->bnrhqk', qb, kb, preferred_element_type=jnp.float32) * (d ** -0.5)
    sc = jnp.where(valid[None, :, None, None], sc + bias[None, None, None], -jnp.inf)
    m = jnp.max(sc, axis=-1, keepdims=True)
    e = jnp.exp(sc - m)
    den = jnp.sum(e, axis=-1, keepdims=True)
    lse = (m + jnp.log(den))[..., 0]
    p = (e / den).astype(v.dtype)
    o = jnp.einsum('bnrhqk,bnkrhd->bnqrhd', p, vb).reshape(b, s, h, d)
    lse = jnp.transpose(lse, (0, 1, 4, 2, 3)).reshape(b, s, h)
    return o, lse


def dilated_attention(c_q, c_k, c_v, slopes):
    b, s, _ = c_q.shape
    q = c_q.reshape(b, s, C_HEADS, HEAD_DIM)
    k = c_k.reshape(b, s, C_HEADS, HEAD_DIM)
    v = c_v.reshape(b, s, C_HEADS, HEAD_DIM)
    outs, lses = [], []
    for p_idx, (window, dil) in enumerate(C_PATTERNS):
        lo, hi = p_idx * C_HEADS_PER_PATTERN, (p_idx + 1) * C_HEADS_PER_PATTERN
        o, lse = strided_band_attention(q[:, :, lo:hi], k[:, :, lo:hi], v[:, :, lo:hi],
                                        slopes[lo:hi], window // (2 * dil), dil)
        outs.append(o)
        lses.append(lse)
    o = jnp.stack(outs, axis=0)
    w = jax.nn.softmax(jnp.stack(lses, axis=0), axis=0)
    return jnp.einsum('pbsh,pbshd->bshd', w.astype(o.dtype), o).reshape(b, s, C_OUT_W)


def parallel_mixer(u, slopes, w_mix_in, a_q_norm, a_k_norm, w_a_proj, w_b_proj, w_c_proj,
                   w_branch_gate, b_branch_gate, w_mix_out):
    b, s, _ = u.shape
    a_q, a_k, a_v, b_in, c_q, c_k, c_v = jnp.split(u @ w_mix_in, MIX_SPLITS, axis=-1)
    y_a = axial_gqa(a_q, a_k, a_v, a_q_norm, a_k_norm) @ w_a_proj
    y_b = fourier_mix(b_in) @ w_b_proj
    y_c = dilated_attention(c_q, c_k, c_v, slopes) @ w_c_proj
    g = jax.nn.sigmoid(u @ w_branch_gate + b_branch_gate).reshape(b, s, N_BRANCH, D_MODEL)
    merged = g[:, :, 0] * y_a + g[:, :, 1] * y_b + g[:, :, 2] * y_c
    return merged @ w_mix_out


def memory_attention(u, mem, mem_norm, w_xq, w_xkv, w_xo):
    b, s, _ = u.shape
    n_mem = mem.shape[1]
    m = rms_norm(mem, mem_norm)
    q = (u @ w_xq).reshape(b, s, X_HEADS, HEAD_DIM)
    kv = (m @ w_xkv).reshape(b, n_mem, 2, X_HEADS, HEAD_DIM)
    k, v = kv[:, :, 0], kv[:, :, 1]
    sc = jnp.einsum('bqhd,bkhd->bhqk', q, k, preferred_element_type=jnp.float32) * (HEAD_DIM ** -0.5)
    p = jax.nn.softmax(sc, axis=-1).astype(v.dtype)
    return jnp.einsum('bhqk,bkhd->bqhd', p, v).reshape(b, s, X_W) @ w_xo


def encoder_layer(x, mem, slopes, *, ffn1_pre_norm, ffn1_w_in, ffn1_w_out, ffn1_post_norm,
                  mix_pre_norm, w_mix_in, a_q_norm, a_k_norm, w_a_proj, w_b_proj, w_c_proj,
                  w_branch_gate, b_branch_gate, w_mix_out, mix_post_norm,
                  xattn_pre_norm, mem_norm, w_xq, w_xkv, w_xo, xattn_post_norm,
                  ffn2_pre_norm, ffn2_w_in, ffn2_w_out, ffn2_post_norm):
    x = x + 0.5 * rms_norm(swiglu(rms_norm(x, ffn1_pre_norm), ffn1_w_in, ffn1_w_out), ffn1_post_norm)
    u = rms_norm(x, mix_pre_norm)
    x = x + rms_norm(parallel_mixer(u, slopes, w_mix_in, a_q_norm, a_k_norm, w_a_proj, w_b_proj,
                                    w_c_proj, w_branch_gate, b_branch_gate, w_mix_out), mix_post_norm)
    u = rms_norm(x, xattn_pre_norm)
    x = x + rms_norm(memory_attention(u, mem, mem_norm, w_xq, w_xkv, w_xo), xattn_post_norm)
    x = x + 0.5 * rms_norm(swiglu(rms_norm(x, ffn2_pre_norm), ffn2_w_in, ffn2_w_out), ffn2_post_norm)
    return x


def run_trunk(x, mem, weights):
    slopes = alibi_slopes(C_HEADS)
    for layer in range(DEPTH):
        x = encoder_layer(x, mem, slopes, **{name: w[layer] for name, w in weights.items()})
    return x


def setup_inputs(seed: int = 0) -> dict:
    key = jax.random.key(seed)
    keys = iter(jax.random.split(key, 40))

    def nrm(shape, fan_in):
        return jax.random.normal(next(keys), shape, jnp.float32) * (fan_in ** -0.5)

    def gain(shape):
        return 1.0 + 0.05 * jax.random.normal(next(keys), shape, jnp.float32)

    L, D, F = DEPTH, D_MODEL, D_FF
    return {
        "x_prompt": jax.random.normal(next(keys), (BATCH, SEQ, D), jnp.float32),
        "x_sample": jax.random.normal(next(keys), (DEC_BATCH, DEC_SEQ, D), jnp.float32),
        "mem_prompt": jax.random.normal(next(keys), (BATCH, N_MEM, D), jnp.float32),
        "mem_sample": jax.random.normal(next(keys), (DEC_BATCH, N_MEM, D), jnp.float32),
        "ffn1_pre_norm": gain((L, D)),
        "ffn1_w_in": nrm((L, D, 2 * F), D),
        "ffn1_w_out": nrm((L, F, D), F),
        "ffn1_post_norm": gain((L, D)),
        "mix_pre_norm": gain((L, D)),
        "w_mix_in": nrm((L, D, MIX_IN_W), D),
        "a_q_norm": gain((L, HEAD_DIM)),
        "a_k_norm": gain((L, HEAD_DIM)),
        "w_a_proj": nrm((L, A_Q_W, D), A_Q_W),
        "w_b_proj": nrm((L, B_W, D), B_W),
        "w_c_proj": nrm((L, C_OUT_W, D), C_OUT_W),
        "w_branch_gate": nrm((L, D, N_BRANCH * D), D),
        "b_branch_gate": 0.01 * jax.random.normal(next(keys), (L, N_BRANCH * D), jnp.float32),
        "w_mix_out": nrm((L, D, D), D),
        "mix_post_norm": gain((L, D)),
        "xattn_pre_norm": gain((L, D)),
        "mem_norm": gain((L, D)),
        "w_xq": nrm((L, D, X_W), D),
        "w_xkv": nrm((L, D, 2 * X_W), D),
        "w_xo": nrm((L, X_W, D), X_W),
        "xattn_post_norm": gain((L, D)),
        "ffn2_pre_norm": gain((L, D)),
        "ffn2_w_in": nrm((L, D, 2 * F), D),
        "ffn2_w_out": nrm((L, F, D), F),
        "ffn2_post_norm": gain((L, D)),
    }


def reference(x_prompt, x_sample, mem_prompt, mem_sample, ffn1_pre_norm, ffn1_w_in, ffn1_w_out,
              ffn1_post_norm, mix_pre_norm, w_mix_in, a_q_norm, a_k_norm, w_a_proj, w_b_proj,
              w_c_proj, w_branch_gate, b_branch_gate, w_mix_out, mix_post_norm, xattn_pre_norm,
              mem_norm, w_xq, w_xkv, w_xo, xattn_post_norm, ffn2_pre_norm, ffn2_w_in, ffn2_w_out,
              ffn2_post_norm):
    weights = dict(
        ffn1_pre_norm=ffn1_pre_norm, ffn1_w_in=ffn1_w_in, ffn1_w_out=ffn1_w_out,
        ffn1_post_norm=ffn1_post_norm, mix_pre_norm=mix_pre_norm, w_mix_in=w_mix_in,
        a_q_norm=a_q_norm, a_k_norm=a_k_norm, w_a_proj=w_a_proj, w_b_proj=w_b_proj,
        w_c_proj=w_c_proj, w_branch_gate=w_branch_gate, b_branch_gate=b_branch_gate,
        w_mix_out=w_mix_out, mix_post_norm=mix_post_norm, xattn_pre_norm=xattn_pre_norm,
        mem_norm=mem_norm, w_xq=w_xq, w_xkv=w_xkv, w_xo=w_xo, xattn_post_norm=xattn_post_norm,
        ffn2_pre_norm=ffn2_pre_norm, ffn2_w_in=ffn2_w_in, ffn2_w_out=ffn2_w_out,
        ffn2_post_norm=ffn2_post_norm)
    y_prompt = run_trunk(x_prompt, mem_prompt, weights)
    y_sample = run_trunk(x_sample, mem_sample, weights)
    return (y_prompt, y_sample)
```

```python
import functools
import math

import numpy as np
import jax
import jax.numpy as jnp
from jax import lax
from jax.experimental import pallas as pl
from jax.experimental.pallas import tpu as pltpu

F32 = jnp.float32
BF16 = jnp.bfloat16

HEAD = 128
GRID_W = 64
EPS = 1e-6
ROPE_BASE = 10000.0
A_Q_HEADS = 12
A_KV_HEADS = 4
A_GROUPS = A_Q_HEADS // A_KV_HEADS
B_GROUPS = 20
C_PATTERNS = ((128, 1), (512, 4), (2048, 16))
C_PER = 4
C_HEADS = C_PER * len(C_PATTERNS)
ALIBI_MAX_EXP = 8.0
X_HEADS = 4
N_BRANCH = 3

A_Q_W = A_Q_HEADS * HEAD
A_KV_W = A_KV_HEADS * HEAD
QK_W = A_Q_W + A_KV_W
B_W = B_GROUPS * HEAD
C_W = C_HEADS * HEAD
C_OUT_W = C_PER * HEAD
X_W = X_HEADS * HEAD
MIX_W = A_Q_W + 2 * A_KV_W + B_W + 3 * C_W
OFF_V = QK_W
OFF_B = QK_W + A_KV_W
OFF_CQ = OFF_B + B_W
OFF_CK = OFF_CQ + C_W
OFF_CV = OFF_CK + C_W
SCALE = HEAD ** -0.5
NEG = -1e30

VMEM_LIMIT_V7X = 56 * 1024 * 1024
LANE = 128


def _params(n_axes):
    return pltpu.CompilerParams(dimension_semantics=("arbitrary",) * n_axes,
                                vmem_limit_bytes=VMEM_LIMIT_V7X)


def _pick(n, pref, unit=LANE):
    if n <= pref:
        return n
    best = None
    for t in range(unit, pref + 1, unit):
        if n % t == 0:
            best = t
    assert best is not None, (n, pref)
    return best


def _rms(x, g):
    return x * lax.rsqrt(jnp.mean(x * x, axis=-1, keepdims=True) + EPS) * g


def _sigmoid(x):
    return 1.0 / (1.0 + jnp.exp(-x))


def _dot(a, b):
    return jnp.dot(a, b, preferred_element_type=F32)


def _dot_nt(a, b):
    return lax.dot_general(a, b, (((1,), (1,)), ((), ())), preferred_element_type=F32)


def _rms_kernel(x_ref, g_ref, o_ref):
    o_ref[...] = _rms(x_ref[...], g_ref[...]).astype(o_ref.dtype)


def rms_rows(x, g, tr=256):
    t, d = x.shape
    tr = _pick(t, tr, 8)
    return pl.pallas_call(
        _rms_kernel,
        grid=(t // tr,),
        in_specs=[pl.BlockSpec((tr, d), lambda i: (i, 0)),
                  pl.BlockSpec((1, d), lambda i: (0, 0))],
        out_specs=pl.BlockSpec((tr, d), lambda i: (i, 0)),
        out_shape=jax.ShapeDtypeStruct((t, d), BF16),
        compiler_params=_params(1),
        name="rms_rows",
    )(x, g.reshape(1, d))


def _resid_kernel(o_ref, x_ref, gp_ref, gn_ref, xo_ref, h_ref, *, coef):
    xn = x_ref[...] + coef * _rms(o_ref[...], gp_ref[...])
    xo_ref[...] = xn
    h_ref[...] = _rms(xn, gn_ref[...]).astype(h_ref.dtype)


def _resid_last_kernel(o_ref, x_ref, gp_ref, xo_ref, *, coef):
    xo_ref[...] = x_ref[...] + coef * _rms(o_ref[...], gp_ref[...])


def resid_norm(o, x, g_post, g_next, coef, tr=256):
    t, d = x.shape
    tr = _pick(t, tr, 8)
    row = pl.BlockSpec((tr, d), lambda i: (i, 0))
    vec = pl.BlockSpec((1, d), lambda i: (0, 0))
    if g_next is None:
        return pl.pallas_call(
            functools.partial(_resid_last_kernel, coef=coef),
            grid=(t // tr,),
            in_specs=[row, row, vec],
            out_specs=row,
            out_shape=jax.ShapeDtypeStruct((t, d), F32),
            compiler_params=_params(1),
            name="resid_last",
        )(o, x, g_post.reshape(1, d)), None
    return pl.pallas_call(
        functools.partial(_resid_kernel, coef=coef),
        grid=(t // tr,),
        in_specs=[row, row, vec, vec],
        out_specs=[row, row],
        out_shape=[jax.ShapeDtypeStruct((t, d), F32), jax.ShapeDtypeStruct((t, d), BF16)],
        compiler_params=_params(1),
        name="resid_norm",
    )(o, x, g_post.reshape(1, d), g_next.reshape(1, d))


def _mm_kernel(x_ref, w_ref, o_ref):
    o_ref[...] = _dot(x_ref[...], w_ref[...]).astype(o_ref.dtype)


def matmul(x, w, out_dtype, tm=1024, tn=1024, name="matmul"):
    m, k = x.shape
    n = w.shape[1]
    tm = _pick(m, tm, 8)
    tn = _pick(n, tn)
    return pl.pallas_call(
        _mm_kernel,
        grid=(m // tm, n // tn),
        in_specs=[pl.BlockSpec((tm, k), lambda i, j: (i, 0)),
                  pl.BlockSpec((k, tn), lambda i, j: (0, j))],
        out_specs=pl.BlockSpec((tm, tn), lambda i, j: (i, j)),
        out_shape=jax.ShapeDtypeStruct((m, n), out_dtype),
        compiler_params=_params(2),
        name=name,
    )(x, w)


def _swiglu_kernel(x_ref, wg_ref, wu_ref, o_ref):
    x = x_ref[...]
    g = _dot(x, wg_ref[...])
    u = _dot(x, wu_ref[...])
    o_ref[...] = (g * _sigmoid(g) * u).astype(o_ref.dtype)


def swiglu_in(h, w_in, tm=1024, tn=256):
    t, d = h.shape
    f = w_in.shape[1] // 2
    tm = _pick(t, tm, 8)
    tn = _pick(f, tn)
    nj = f // tn
    return pl.pallas_call(
        _swiglu_kernel,
        grid=(t // tm, nj),
        in_specs=[pl.BlockSpec((tm, d), lambda i, j: (i, 0)),
                  pl.BlockSpec((d, tn), lambda i, j: (0, j)),
                  pl.BlockSpec((d, tn), lambda i, j: (0, nj + j))],
        out_specs=pl.BlockSpec((tm, tn), lambda i, j: (i, j)),
        out_shape=jax.ShapeDtypeStruct((t, f), BF16),
        compiler_params=_params(2),
        name="swiglu_in",
    )(h, w_in, w_in)


def _merge_kernel(u_ref, a_ref, b_ref, c_ref, wg0_ref, wg1_ref, wg2_ref, bg0_ref, bg1_ref, bg2_ref,
                  wa_ref, wb_ref, wc_ref, o_ref):
    u = u_ref[...]
    acc = _sigmoid(_dot(u, wg0_ref[...]) + bg0_ref[...]) * _dot(a_ref[...], wa_ref[...])
    acc += _sigmoid(_dot(u, wg1_ref[...]) + bg1_ref[...]) * _dot(b_ref[...], wb_ref[...])
    acc += _sigmoid(_dot(u, wg2_ref[...]) + bg2_ref[...]) * _dot(c_ref[...], wc_ref[...])
    o_ref[...] = acc.astype(o_ref.dtype)


def merge_branches(u, a, b, c, w_gate, b_gate, w_a, w_b, w_c, tm=512, tn=256):
    t, d = u.shape
    tm = _pick(t, tm, 8)
    tn = _pick(d, tn)
    nj = d // tn
    act = lambda width: pl.BlockSpec((tm, width), lambda i, j: (i, 0))
    gate_w = lambda br: pl.BlockSpec((d, tn), lambda i, j: (0, br * nj + j))
    gate_b = lambda br: pl.BlockSpec((1, tn), lambda i, j: (0, br * nj + j))
    proj_w = lambda width: pl.BlockSpec((width, tn), lambda i, j: (0, j))
    bg = b_gate.reshape(1, N_BRANCH * d)
    return pl.pallas_call(
        _merge_kernel,
        grid=(t // tm, nj),
        in_specs=[act(d), act(A_Q_W), act(B_W), act(C_OUT_W),
                  gate_w(0), gate_w(1), gate_w(2), gate_b(0), gate_b(1), gate_b(2),
                  proj_w(A_Q_W), proj_w(B_W), proj_w(C_OUT_W)],
        out_specs=pl.BlockSpec((tm, tn), lambda i, j: (i, j)),
        out_shape=jax.ShapeDtypeStruct((t, d), BF16),
        compiler_params=_params(2),
        name="merge_branches",
    )(u, a, b, c, w_gate, w_gate, w_gate, bg, bg, bg, w_a, w_b, w_c)


def _qk_prep_kernel(x_ref, g_ref, sc_ref, cs_ref, sn_ref, o_ref):
    cs = cs_ref[...]
    sn = sn_ref[...]
    for h in range(A_Q_HEADS + A_KV_HEADS):
        sl = slice(h * HEAD, (h + 1) * HEAD)
        x = _rms(x_ref[:, sl].astype(F32), g_ref[h:h + 1, :])
        y = x * cs + pltpu.roll(x, HEAD // 2, 1) * sn
        o_ref[:, sl] = (y * sc_ref[h:h + 1, :]).astype(o_ref.dtype)


def qk_prep(proj, gains, scales, cs, sn, groups, tr=256):
    t = proj.shape[0]
    (_, bp, sp), (_, _, ss) = groups
    tr = _pick(math.gcd(sp, ss), tr, 8)
    n_p = bp * sp // tr

    def pos_block(i):
        return jnp.where(i < n_p, i % (sp // tr), (i - n_p) % (ss // tr))

    nh = A_Q_HEADS + A_KV_HEADS
    return pl.pallas_call(
        _qk_prep_kernel,
        grid=(t // tr,),
        in_specs=[pl.BlockSpec((tr, QK_W), lambda i: (i, 0)),
                  pl.BlockSpec((nh, HEAD), lambda i: (0, 0)),
                  pl.BlockSpec((nh, HEAD), lambda i: (0, 0)),
                  pl.BlockSpec((tr, HEAD), lambda i: (pos_block(i), 0)),
                  pl.BlockSpec((tr, HEAD), lambda i: (pos_block(i), 0))],
        out_specs=pl.BlockSpec((tr, QK_W), lambda i: (i, 0)),
        out_shape=jax.ShapeDtypeStruct((t, QK_W), BF16),
        compiler_params=_params(1),
        name="qk_prep",
    )(proj, gains, scales, cs, sn)


def _attn_a_kernel(q_ref, k_ref, v_ref, o_ref):
    k = k_ref[...]
    v = v_ref[...]
    for g in range(A_GROUPS):
        sl = slice(g * HEAD, (g + 1) * HEAD)
        s = _dot_nt(q_ref[:, sl], k)
        p = jnp.exp(s - jnp.max(s, axis=-1, keepdims=True))
        l = jnp.sum(p, axis=-1, keepdims=True)
        o = _dot(p.astype(BF16), v) / l
        o_ref[:, sl] = o.astype(o_ref.dtype)


def attn_a(qk, proj, group, tq=256):
    row0, nseq, s = group
    tq = _pick(s, tq, 8)
    nq = s // tq
    rb0, sb0 = row0 // tq, row0 // s
    gw = A_GROUPS * HEAD
    return pl.pallas_call(
        _attn_a_kernel,
        grid=(nseq, A_KV_HEADS, nq),
        in_specs=[pl.BlockSpec((tq, gw), lambda b, h, i: (rb0 + b * nq + i, h)),
                  pl.BlockSpec((s, HEAD), lambda b, h, i: (sb0 + b, A_Q_HEADS + h)),
                  pl.BlockSpec((s, HEAD), lambda b, h, i: (sb0 + b, OFF_V // HEAD + h))],
        out_specs=pl.BlockSpec((tq, gw), lambda b, h, i: (b * nq + i, h)),
        out_shape=jax.ShapeDtypeStruct((nseq * s, A_Q_W), BF16),
        compiler_params=_params(3),
        name="attn_a",
    )(qk, qk, proj)


def _chan_dft_kernel(z_ref, cs_ref, zc_ref, zs_ref):
    cs = cs_ref[...]
    for g in range(B_GROUPS):
        sl = slice(g * HEAD, (g + 1) * HEAD)
        y = _dot(z_ref[:, sl], cs) * (HEAD ** -0.5)
        zc_ref[:, sl] = y[:, :HEAD].astype(zc_ref.dtype)
        zs_ref[:, sl] = y[:, HEAD:].astype(zs_ref.dtype)


def chan_dft(proj, cs128, tr=512):
    t = proj.shape[0]
    tr = _pick(t, tr, 8)
    row = pl.BlockSpec((tr, B_W), lambda i: (i, 0))
    return pl.pallas_call(
        _chan_dft_kernel,
        grid=(t // tr,),
        in_specs=[pl.BlockSpec((tr, B_W), lambda i: (i, OFF_B // B_W)),
                  pl.BlockSpec((HEAD, 2 * HEAD), lambda i: (0, 0))],
        out_specs=[row, row],
        out_shape=[jax.ShapeDtypeStruct((t, B_W), BF16)] * 2,
        compiler_params=_params(1),
        name="chan_dft",
    )(proj, cs128)


def _seq_dft_kernel(c_ref, s_ref, zc_ref, zs_ref, o_ref, *, scale):
    y = _dot(c_ref[...], zc_ref[...]) - _dot(s_ref[...], zs_ref[...])
    o_ref[...] = (y * scale).astype(o_ref.dtype)


def seq_dft(zc, zs, cos_t, sin_t, group, tm=512, tn=512):
    row0, nseq, s = group
    tm = _pick(s, tm, 8)
    tn = _pick(B_W, tn)
    ni, nj = s // tm, B_W // tn
    sb0 = row0 // s
    return pl.pallas_call(
        functools.partial(_seq_dft_kernel, scale=s ** -0.5),
        grid=(nseq, nj, ni),
        in_specs=[pl.BlockSpec((tm, s), lambda b, j, i: (i, 0)),
                  pl.BlockSpec((tm, s), lambda b, j, i: (i, 0)),
                  pl.BlockSpec((s, tn), lambda b, j, i: (sb0 + b, j)),
                  pl.BlockSpec((s, tn), lambda b, j, i: (sb0 + b, j))],
        out_specs=pl.BlockSpec((tm, tn), lambda b, j, i: (b * ni + i, j)),
        out_shape=jax.ShapeDtypeStruct((nseq * s, B_W), BF16),
        compiler_params=_params(3),
        name="seq_dft",
    )(cos_t, sin_t, zc, zs)


def _dft_tables(n):
    k = lax.broadcasted_iota(jnp.int32, (n, n), 0)
    t = lax.broadcasted_iota(jnp.int32, (n, n), 1)
    ang = ((k * t) % n).astype(F32) * (2.0 * math.pi / n)
    return jnp.cos(ang).astype(BF16), jnp.sin(ang).astype(BF16)


def _dil_kernel(q_ref, kp_ref, kc_ref, kn_ref, vp_ref, vc_ref, vn_ref, o_ref, l_ref, *,
                dil, half, sub, slopes):
    i = pl.program_id(1)
    qi = lax.broadcasted_iota(jnp.int32, (HEAD, 3 * HEAD), 0)
    ci = lax.broadcasted_iota(jnp.int32, (HEAD, 3 * HEAD), 1)
    rel = ci - HEAD - qi
    kpos = i * HEAD + ci - HEAD
    valid = (jnp.abs(rel) <= half) & (kpos >= 0) & (kpos < sub)
    dist = (jnp.abs(rel) * dil).astype(F32)
    for h in range(C_PER):
        sl = slice(h * HEAD, (h + 1) * HEAD)
        q = q_ref[:, sl]
        s = jnp.concatenate([_dot_nt(q, kp_ref[:, sl]), _dot_nt(q, kc_ref[:, sl]),
                             _dot_nt(q, kn_ref[:, sl])], axis=-1)
        s = jnp.where(valid, s * SCALE - slopes[h] * dist, NEG)
        m = jnp.max(s, axis=-1, keepdims=True)
        e = jnp.exp(s - m)
        den = jnp.sum(e, axis=-1, keepdims=True)
        p = (e / den).astype(BF16)
        o = (_dot(p[:, :HEAD], vp_ref[:, sl]) + _dot(p[:, HEAD:2 * HEAD], vc_ref[:, sl])
             + _dot(p[:, 2 * HEAD:], vn_ref[:, sl]))
        o_ref[:, sl] = o
        l_ref[:, sl] = jnp.broadcast_to(m + jnp.log(den), (HEAD, HEAD))


def dil_attn(proj, group, p_idx, slopes):
    window, dil = C_PATTERNS[p_idx]
    half = window // (2 * dil)
    assert half <= HEAD // 2
    row0, nseq, s = group
    t = proj.shape[0]
    sub = s // dil
    assert sub % HEAD == 0 and row0 % (dil * HEAD) == 0
    nb = sub // HEAD
    rb0 = row0 // dil // HEAD
    pv = proj.reshape(t // dil, dil * MIX_W)
    per_res = MIX_W // C_OUT_W
    cq, ck, cv = OFF_CQ // C_OUT_W + p_idx, OFF_CK // C_OUT_W + p_idx, OFF_CV // C_OUT_W + p_idx

    def spec(col, shift):
        def imap(b, i, r):
            ii = jnp.clip(i + shift, 0, nb - 1)
            return (rb0 + b * nb + ii, r * per_res + col)
        return pl.BlockSpec((HEAD, C_OUT_W), imap)

    out_spec = pl.BlockSpec((HEAD, C_OUT_W), lambda b, i, r: (b * nb + i, r))
    out_sds = jax.ShapeDtypeStruct((nseq * sub, dil * C_OUT_W), F32)
    o, lse = pl.pallas_call(
        functools.partial(_dil_kernel, dil=dil, half=half, sub=sub,
                          slopes=tuple(slopes[p_idx * C_PER:(p_idx + 1) * C_PER])),
        grid=(nseq, nb, dil),
        in_specs=[spec(cq, 0), spec(ck, -1), spec(ck, 0), spec(ck, 1),
                  spec(cv, -1), spec(cv, 0), spec(cv, 1)],
        out_specs=[out_spec, out_spec],
        out_shape=[out_sds, out_sds],
        compiler_params=_params(3),
        name=f"dil_attn_p{p_idx}",
    )(pv, pv, pv, pv, pv, pv, pv)
    return o.reshape(nseq * s, C_OUT_W), lse.reshape(nseq * s, C_OUT_W)


def _dil_combine_kernel(o0, o1, o2, l0, l1, l2, out_ref):
    a0, a1, a2 = l0[...], l1[...], l2[...]
    m = jnp.maximum(jnp.maximum(a0, a1), a2)
    e0, e1, e2 = jnp.exp(a0 - m), jnp.exp(a1 - m), jnp.exp(a2 - m)
    tot = e0 + e1 + e2
    out_ref[...] = ((e0 * o0[...] + e1 * o1[...] + e2 * o2[...]) / tot).astype(out_ref.dtype)


def dil_combine(outs, lses, tr=512):
    t = outs[0].shape[0]
    tr = _pick(t, tr, 8)
    row = pl.BlockSpec((tr, C_OUT_W), lambda i: (i, 0))
    return pl.pallas_call(
        _dil_combine_kernel,
        grid=(t // tr,),
        in_specs=[row] * 6,
        out_specs=row,
        out_shape=jax.ShapeDtypeStruct((t, C_OUT_W), BF16),
        compiler_params=_params(1),
        name="dil_combine",
    )(*outs, *lses)


def _xattn_kernel(q_ref, kv_ref, o_ref):
    for h in range(X_HEADS):
        sl = slice(h * HEAD, (h + 1) * HEAD)
        s = _dot_nt(q_ref[:, sl], kv_ref[:, sl]) * SCALE
        p = jnp.exp(s - jnp.max(s, axis=-1, keepdims=True))
        l = jnp.sum(p, axis=-1, keepdims=True)
        o = _dot(p.astype(BF16), kv_ref[:, X_W + h * HEAD:X_W + (h + 1) * HEAD]) / l
        o_ref[:, sl] = o.astype(o_ref.dtype)


def xattn(q, kv, groups, n_mem, tq=512):
    t = q.shape[0]
    (_, bp, sp), (_, _, ss) = groups
    tq = _pick(math.gcd(sp, ss), tq, 8)
    n_p = bp * sp // tq

    def seq_of(i):
        return jnp.where(i < n_p, i // (sp // tq), bp + (i - n_p) // (ss // tq))

    return pl.pallas_call(
        _xattn_kernel,
        grid=(t // tq,),
        in_specs=[pl.BlockSpec((tq, X_W), lambda i: (i, 0)),
                  pl.BlockSpec((n_mem, 2 * X_W), lambda i: (seq_of(i), 0))],
        out_specs=pl.BlockSpec((tq, X_W), lambda i: (i, 0)),
        out_shape=jax.ShapeDtypeStruct((t, X_W), BF16),
        compiler_params=_params(1),
        name="xattn",
    )(q, kv)


def _rope_tables(seq_len):
    t = jnp.arange(seq_len)
    row = (t // GRID_W).astype(F32)
    col = (t % GRID_W).astype(F32)
    n_pairs = HEAD // 4
    inv_freq = ROPE_BASE ** (-jnp.arange(n_pairs, dtype=F32) / n_pairs)
    ang = jnp.concatenate([row[:, None] * inv_freq[None, :], col[:, None] * inv_freq[None, :]], axis=-1)
    cos, sin = jnp.cos(ang), jnp.sin(ang)
    return jnp.concatenate([cos, cos], axis=-1), jnp.concatenate([-sin, sin], axis=-1)


_PAIR_PERM = np.concatenate([np.arange(0, HEAD, 2), np.arange(1, HEAD, 2)])


def _qk_col_perm():
    perm = np.arange(MIX_W)
    for h in range(A_Q_HEADS + A_KV_HEADS):
        perm[h * HEAD:(h + 1) * HEAD] = h * HEAD + _PAIR_PERM
    return perm


def _alibi_slopes():
    h = np.arange(1, C_HEADS + 1, dtype=np.float32)
    return [float(v) for v in np.float32(2.0) ** (-np.float32(ALIBI_MAX_EXP) * h / np.float32(C_HEADS))]


def kernel(x_prompt, x_sample, mem_prompt, mem_sample, ffn1_pre_norm, ffn1_w_in, ffn1_w_out, ffn1_post_norm, mix_pre_norm, w_mix_in, a_q_norm, a_k_norm, w_a_proj, w_b_proj, w_c_proj, w_branch_gate, b_branch_gate, w_mix_out, mix_post_norm, xattn_pre_norm, mem_norm, w_xq, w_xkv, w_xo, xattn_post_norm, ffn2_pre_norm, ffn2_w_in, ffn2_w_out, ffn2_post_norm):
    bp, sp, d = x_prompt.shape
    bs, ss, _ = x_sample.shape
    n_mem = mem_prompt.shape[1]
    depth = ffn1_w_in.shape[0]
    tp = bp * sp
    groups = ((0, bp, sp), (tp, bs, ss))
    assert tp % ss == 0 and sp % GRID_W == 0 and ss % GRID_W == 0

    x = jnp.concatenate([x_prompt.reshape(tp, d), x_sample.reshape(bs * ss, d)], axis=0)
    mem = jnp.concatenate([mem_prompt.reshape(bp * n_mem, d), mem_sample.reshape(bs * n_mem, d)], axis=0)

    rope_cs, rope_sn = _rope_tables(max(sp, ss))
    dft = {s: _dft_tables(s) for s in {sp, ss}}
    kc = np.arange(HEAD)
    ang128 = 2.0 * np.pi * ((kc[:, None] * kc[None, :]) % HEAD) / HEAD
    cs128 = jnp.asarray(np.concatenate([np.cos(ang128), np.sin(ang128)], axis=1), BF16)
    slopes = _alibi_slopes()
    qk_perm = _qk_col_perm()
    head_scale = jnp.concatenate([jnp.full((A_Q_HEADS, HEAD), SCALE, F32),
                                  jnp.ones((A_KV_HEADS, HEAD), F32)], axis=0)

    h = rms_rows(x, ffn1_pre_norm[0])
    for l in range(depth):
        bf = lambda w: w[l].astype(BF16)

        act = swiglu_in(h, bf(ffn1_w_in))
        o = matmul(act, bf(ffn1_w_out), F32, tm=512, tn=512, name="ffn_out")
        x, u = resid_norm(o, x, ffn1_post_norm[l], mix_pre_norm[l], 0.5)

        proj = matmul(u, w_mix_in[l][:, qk_perm].astype(BF16), BF16, name="mix_in")
        gains = jnp.concatenate([jnp.tile(a_q_norm[l][_PAIR_PERM][None], (A_Q_HEADS, 1)),
                                 jnp.tile(a_k_norm[l][_PAIR_PERM][None], (A_KV_HEADS, 1))], axis=0)
        qk = qk_prep(proj, gains, head_scale, rope_cs, rope_sn, groups)
        y_a = jnp.concatenate([attn_a(qk, proj, g) for g in groups], axis=0)
        zc, zs = chan_dft(proj, cs128)
        y_b = jnp.concatenate([seq_dft(zc, zs, *dft[g[2]], g) for g in groups], axis=0)
        outs, lses = [], []
        for p_idx in range(len(C_PATTERNS)):
            parts = [dil_attn(proj, g, p_idx, slopes) for g in groups]
            outs.append(jnp.concatenate([p[0] for p in parts], axis=0))
            lses.append(jnp.concatenate([p[1] for p in parts], axis=0))
        y_c = dil_combine(outs, lses)
        merged = merge_branches(u, y_a, y_b, y_c, bf(w_branch_gate), b_branch_gate[l],
                                bf(w_a_proj), bf(w_b_proj), bf(w_c_proj))
        o = matmul(merged, bf(w_mix_out), F32, name="mix_out")
        x, u = resid_norm(o, x, mix_post_norm[l], xattn_pre_norm[l], 1.0)

        q = matmul(u, bf(w_xq), BF16, name="xattn_q")
        kv = matmul(rms_rows(mem, mem_norm[l]), bf(w_xkv), BF16, name="xattn_kv")
        o = matmul(xattn(q, kv, groups, n_mem), bf(w_xo), F32, name="xattn_o")
        x, h = resid_norm(o, x, xattn_post_norm[l], ffn2_pre_norm[l], 1.0)

        act = swiglu_in(h, bf(ffn2_w_in))
        o = matmul(act, bf(ffn2_w_out), F32, tm=512, tn=512, name="ffn_out")
        g_next = ffn1_pre_norm[l + 1] if l + 1 < depth else None
        x, h = resid_norm(o, x, ffn2_post_norm[l], g_next, 0.5)

    return x[:tp].reshape(bp, sp, d), x[tp:].reshape(bs, ss, d)
```

```python
import functools
import math

import numpy as np
import jax
import jax.numpy as jnp
from jax import lax
from jax.experimental import pallas as pl
from jax.experimental.pallas import tpu as pltpu

F32 = jnp.float32
BF16 = jnp.bfloat16

HEAD = 128
GRID_W = 64
EPS = 1e-6
ROPE_BASE = 10000.0
A_Q_HEADS = 12
A_KV_HEADS = 4
A_GROUPS = A_Q_HEADS // A_KV_HEADS
B_GROUPS = 20
C_PATTERNS = ((128, 1), (512, 4), (2048, 16))
C_PER = 4
C_HEADS = C_PER * len(C_PATTERNS)
ALIBI_MAX_EXP = 8.0
X_HEADS = 4
N_BRANCH = 3

A_Q_W = A_Q_HEADS * HEAD
A_KV_W = A_KV_HEADS * HEAD
QK_W = A_Q_W + A_KV_W
B_W = B_GROUPS * HEAD
C_W = C_HEADS * HEAD
C_OUT_W = C_PER * HEAD
X_W = X_HEADS * HEAD
MIX_W = A_Q_W + 2 * A_KV_W + B_W + 3 * C_W
BLK_V = QK_W // C_OUT_W
BLK_CQ = (QK_W + A_KV_W + B_W) // C_OUT_W
N_PAT = len(C_PATTERNS)
BLK_B = (QK_W + A_KV_W) // C_OUT_W
NAT_W = B_W + A_KV_W + 3 * C_OUT_W
NAT_V = B_W // C_OUT_W
NAT_C = NAT_V + 1
SCALE = HEAD ** -0.5
NEG = -1e30

VMEM_LIMIT_V7X = 56 * 1024 * 1024
LANE = 128


def _params(n_axes):
    return pltpu.CompilerParams(dimension_semantics=("arbitrary",) * n_axes,
                                vmem_limit_bytes=VMEM_LIMIT_V7X)


def _pick(n, pref, unit=LANE):
    if n <= pref:
        return n
    best = None
    for t in range(unit, pref + 1, unit):
        if n % t == 0:
            best = t
    assert best is not None, (n, pref)
    return best


def _wspec(layer, shape, imap):
    return pl.BlockSpec((None,) + shape, lambda *g: (layer,) + tuple(imap(*g)))


def _rms(x, g):
    return x * lax.rsqrt(jnp.mean(x * x, axis=-1, keepdims=True) + EPS) * g


def _sigmoid(x):
    return 1.0 / (1.0 + jnp.exp(-x))


def _dot(a, b):
    return jnp.dot(a, b, preferred_element_type=F32)


def _dot_nt(a, b):
    return lax.dot_general(a, b, (((1,), (1,)), ((), ())), preferred_element_type=F32)


def _rms_kernel(x_ref, g_ref, o_ref):
    o_ref[...] = _rms(x_ref[...], g_ref[...]).astype(o_ref.dtype)


def rms_rows(x, g, layer, tr=256):
    t, d = x.shape
    tr = _pick(t, tr, 8)
    return pl.pallas_call(
        _rms_kernel,
        grid=(t // tr,),
        in_specs=[pl.BlockSpec((tr, d), lambda i: (i, 0)),
                  _wspec(layer, (1, d), lambda i: (0, 0))],
        out_specs=pl.BlockSpec((tr, d), lambda i: (i, 0)),
        out_shape=jax.ShapeDtypeStruct((t, d), BF16),
        compiler_params=_params(1),
        name="rms_rows",
    )(x, g.reshape(g.shape[0], 1, d))


def _resid_kernel(o_ref, x_ref, gp_ref, gn_ref, xo_ref, h_ref, *, coef):
    xn = x_ref[...] + coef * _rms(o_ref[...], gp_ref[...])
    xo_ref[...] = xn
    h_ref[...] = _rms(xn, gn_ref[...]).astype(h_ref.dtype)


def _resid_last_kernel(o_ref, x_ref, gp_ref, xo_ref, *, coef):
    xo_ref[...] = x_ref[...] + coef * _rms(o_ref[...], gp_ref[...])


def resid_norm(o, x, g_post, layer, g_next, next_layer, coef, tr=256):
    t, d = x.shape
    tr = _pick(t, tr, 8)
    row = pl.BlockSpec((tr, d), lambda i: (i, 0))
    vec = lambda lyr: _wspec(lyr, (1, d), lambda i: (0, 0))
    g3 = lambda g: g.reshape(g.shape[0], 1, d)
    if g_next is None:
        return pl.pallas_call(
            functools.partial(_resid_last_kernel, coef=coef),
            grid=(t // tr,),
            in_specs=[row, row, vec(layer)],
            out_specs=row,
            out_shape=jax.ShapeDtypeStruct((t, d), F32),
            compiler_params=_params(1),
            name="resid_last",
        )(o, x, g3(g_post)), None
    return pl.pallas_call(
        functools.partial(_resid_kernel, coef=coef),
        grid=(t // tr,),
        in_specs=[row, row, vec(layer), vec(next_layer)],
        out_specs=[row, row],
        out_shape=[jax.ShapeDtypeStruct((t, d), F32), jax.ShapeDtypeStruct((t, d), BF16)],
        compiler_params=_params(1),
        name="resid_norm",
    )(o, x, g3(g_post), g3(g_next))


def _mm_kernel(x_ref, w_ref, o_ref):
    o_ref[...] = _dot(x_ref[...], w_ref[...]).astype(o_ref.dtype)


def matmul(x, w, layer, out_dtype, tm=1024, tn=1024, n_out=None, col_block=None, name="matmul"):
    m, k = x.shape
    n = w.shape[2] if n_out is None else n_out
    tm = _pick(m, tm, 8)
    tn = _pick(n, tn)
    col_block = col_block or (lambda j: j)
    return pl.pallas_call(
        _mm_kernel,
        grid=(m // tm, n // tn),
        in_specs=[pl.BlockSpec((tm, k), lambda i, j: (i, 0)),
                  _wspec(layer, (k, tn), lambda i, j: (0, col_block(j)))],
        out_specs=pl.BlockSpec((tm, tn), lambda i, j: (i, j)),
        out_shape=jax.ShapeDtypeStruct((m, n), out_dtype),
        compiler_params=_params(2),
        name=name,
    )(x, w)


def _swiglu_kernel(x_ref, wg_ref, wu_ref, o_ref):
    x = x_ref[...]
    g = _dot(x, wg_ref[...])
    u = _dot(x, wu_ref[...])
    o_ref[...] = (g * _sigmoid(g) * u).astype(o_ref.dtype)


def swiglu_in(h, w_in, layer, tm=2048, tn=256):
    t, d = h.shape
    f = w_in.shape[2] // 2
    tm = _pick(t, tm, 8)
    tn = _pick(f, tn)
    nj = f // tn
    return pl.pallas_call(
        _swiglu_kernel,
        grid=(t // tm, nj),
        in_specs=[pl.BlockSpec((tm, d), lambda i, j: (i, 0)),
                  _wspec(layer, (d, tn), lambda i, j: (0, j)),
                  _wspec(layer, (d, tn), lambda i, j: (0, nj + j))],
        out_specs=pl.BlockSpec((tm, tn), lambda i, j: (i, j)),
        out_shape=jax.ShapeDtypeStruct((t, f), BF16),
        compiler_params=_params(2),
        name="swiglu_in",
    )(h, w_in, w_in)


def _merge_kernel(u_ref, a_ref, b_ref, c_ref, wg0_ref, wg1_ref, wg2_ref, bg0_ref, bg1_ref, bg2_ref,
                  wa_ref, wb_ref, wc_ref, o_ref):
    u = u_ref[...]
    acc = _sigmoid(_dot(u, wg0_ref[...]) + bg0_ref[...]) * _dot(a_ref[...], wa_ref[...])
    acc += _sigmoid(_dot(u, wg1_ref[...]) + bg1_ref[...]) * _dot(b_ref[...], wb_ref[...])
    acc += _sigmoid(_dot(u, wg2_ref[...]) + bg2_ref[...]) * _dot(c_ref[...], wc_ref[...])
    o_ref[...] = acc.astype(o_ref.dtype)


def merge_branches(u, a, b, c, w_gate, b_gate, w_a, w_b, w_c, layer, tm=512, tn=256):
    t, d = u.shape
    tm = _pick(t, tm, 8)
    tn = _pick(d, tn)
    nj = d // tn
    act = lambda width: pl.BlockSpec((tm, width), lambda i, j: (i, 0))
    gate_w = lambda br: _wspec(layer, (d, tn), lambda i, j: (0, br * nj + j))
    gate_b = lambda br: _wspec(layer, (1, tn), lambda i, j: (0, br * nj + j))
    proj_w = lambda width: _wspec(layer, (width, tn), lambda i, j: (0, j))
    bg = b_gate.reshape(b_gate.shape[0], 1, N_BRANCH * d)
    return pl.pallas_call(
        _merge_kernel,
        grid=(t // tm, nj),
        in_specs=[act(d), act(A_Q_W), act(B_W), act(C_OUT_W),
                  gate_w(0), gate_w(1), gate_w(2), gate_b(0), gate_b(1), gate_b(2),
                  proj_w(A_Q_W), proj_w(B_W), proj_w(C_OUT_W)],
        out_specs=pl.BlockSpec((tm, tn), lambda i, j: (i, j)),
        out_shape=jax.ShapeDtypeStruct((t, d), BF16),
        compiler_params=_params(2),
        name="merge_branches",
    )(u, a, b, c, w_gate, w_gate, w_gate, bg, bg, bg, w_a, w_b, w_c)


def _out_resid_kernel(a_ref, w_ref, x_ref, gp_ref, gn_ref, xo_ref, h_ref):
    xn = x_ref[...] + _rms(_dot(a_ref[...], w_ref[...]), gp_ref[...])
    xo_ref[...] = xn
    h_ref[...] = _rms(xn, gn_ref[...]).astype(h_ref.dtype)


def out_proj_resid(a, w, x, g_post, g_next, layer, tm=256):
    t, d = x.shape
    k = a.shape[1]
    tm = _pick(t, tm, 8)
    row = pl.BlockSpec((tm, d), lambda i: (i, 0))
    vec = _wspec(layer, (1, d), lambda i: (0, 0))
    g3 = lambda g: g.reshape(g.shape[0], 1, d)
    return pl.pallas_call(
        _out_resid_kernel,
        grid=(t // tm,),
        in_specs=[pl.BlockSpec((tm, k), lambda i: (i, 0)),
                  _wspec(layer, (k, d), lambda i: (0, 0)), row, vec, vec],
        out_specs=[row, row],
        out_shape=[jax.ShapeDtypeStruct((t, d), F32), jax.ShapeDtypeStruct((t, d), BF16)],
        compiler_params=_params(1),
        name="out_proj_resid",
    )(a, w, x, g3(g_post), g3(g_next))


def _mix_qk_kernel(u_ref, w_ref, g_ref, sc_ref, cs_ref, sa_ref, sb_ref, o_ref):
    y = _dot(u_ref[...], w_ref[...])
    cs, sa, sb = cs_ref[...], sa_ref[...], sb_ref[...]
    for h in range(o_ref.shape[1] // HEAD):
        sl = slice(h * HEAD, (h + 1) * HEAD)
        x = _rms(y[:, sl], g_ref[h:h + 1, :])
        r = x * cs + pltpu.roll(x, HEAD - 1, 1) * sa + pltpu.roll(x, 1, 1) * sb
        o_ref[:, sl] = (r * sc_ref[h:h + 1, :]).astype(o_ref.dtype)


def mix_qk(u, w_mix_in, layer, gains, scales, rope, groups, tm=1024, tn=1024):
    t, d = u.shape
    (_, bp, sp), (_, _, ss) = groups
    tm = _pick(math.gcd(sp, ss), tm, 8)
    tn = _pick(QK_W, tn)
    n_p = bp * sp // tm
    hpt = tn // HEAD

    def pos_block(i):
        return jnp.where(i < n_p, i % (sp // tm), (i - n_p) % (ss // tm))

    tab = pl.BlockSpec((tm, HEAD), lambda i, j: (pos_block(i), 0))
    return pl.pallas_call(
        _mix_qk_kernel,
        grid=(t // tm, QK_W // tn),
        in_specs=[pl.BlockSpec((tm, d), lambda i, j: (i, 0)),
                  _wspec(layer, (d, tn), lambda i, j: (0, j)),
                  _wspec(layer, (hpt, HEAD), lambda i, j: (j, 0)),
                  pl.BlockSpec((hpt, HEAD), lambda i, j: (j, 0)),
                  tab, tab, tab],
        out_specs=pl.BlockSpec((tm, tn), lambda i, j: (i, j)),
        out_shape=jax.ShapeDtypeStruct((t, QK_W), BF16),
        compiler_params=_params(2),
        name="mix_qk",
    )(u, w_mix_in, gains, scales, *rope)


def _mix_dil_kernel(u_ref, w_ref, o_ref, scr_ref, *, dil):
    y = _dot(u_ref[...], w_ref[...])
    nchunk, tm, _ = scr_ref.shape
    for c in range(nchunk):
        scr_ref[c] = y[:, c * LANE:(c + 1) * LANE]
    for r in range(dil):
        for c in range(nchunk):
            o_ref[r, :, c * LANE:(c + 1) * LANE] = (
                scr_ref[c, pl.ds(r, tm // dil, stride=dil), :].astype(o_ref.dtype))


def mix_dil(u, w_mix_in, layer, p_idx, tm=1024):
    t, d = u.shape
    dil = C_PATTERNS[p_idx][1]
    tm = _pick(t, tm, 8 * dil)
    return pl.pallas_call(
        functools.partial(_mix_dil_kernel, dil=dil),
        grid=(t // tm, 3),
        in_specs=[pl.BlockSpec((tm, d), lambda i, j: (i, 0)),
                  _wspec(layer, (d, C_OUT_W), lambda i, j: (0, BLK_CQ + p_idx + N_PAT * j))],
        out_specs=pl.BlockSpec((dil, tm // dil, C_OUT_W), lambda i, j: (0, i, j)),
        out_shape=jax.ShapeDtypeStruct((dil, t // dil, 3 * C_OUT_W), BF16),
        scratch_shapes=[pltpu.VMEM((C_OUT_W // LANE, tm, LANE), F32)],
        compiler_params=_params(2),
        name=f"mix_dil_p{p_idx}",
    )(u, w_mix_in)


def _attn_a_kernel(q_ref, k_ref, v_ref, *rest):
    o_ref = rest[-1]
    k = k_ref[...]
    v = v_ref[...]
    for g in range(A_GROUPS):
        sl = slice(g * HEAD, (g + 1) * HEAD)
        s = _dot_nt(q_ref[:, sl], k)
        p = jnp.exp(s - jnp.max(s, axis=-1, keepdims=True))
        l = jnp.sum(p, axis=-1, keepdims=True)
        o = _dot(p.astype(BF16), v) / l
        o_ref[:, sl] = o.astype(o_ref.dtype)


def attn_a(qk, nat, group, prev_out, tq=256):
    row0, nseq, s = group
    t = qk.shape[0]
    tq = _pick(s, tq, 8)
    nq = s // tq
    rb0, sb0 = row0 // tq, row0 // s
    gw = A_GROUPS * HEAD
    in_specs = [pl.BlockSpec((tq, gw), lambda b, h, i: (rb0 + b * nq + i, h)),
                pl.BlockSpec((s, HEAD), lambda b, h, i: (sb0 + b, A_Q_HEADS + h)),
                pl.BlockSpec((s, HEAD), lambda b, h, i: (sb0 + b, NAT_V * (C_OUT_W // HEAD) + h))]
    args = [qk, qk, nat]
    aliases = {}
    if prev_out is not None:
        in_specs.append(pl.BlockSpec(memory_space=pl.ANY))
        args.append(prev_out)
        aliases = {3: 0}
    return pl.pallas_call(
        _attn_a_kernel,
        grid=(nseq, A_KV_HEADS, nq),
        in_specs=in_specs,
        out_specs=pl.BlockSpec((tq, gw), lambda b, h, i: (rb0 + b * nq + i, h)),
        out_shape=jax.ShapeDtypeStruct((t, A_Q_W), BF16),
        input_output_aliases=aliases,
        compiler_params=_params(3),
        name="attn_a",
    )(*args)


def _chan_dft_kernel(z_ref, cs_ref, zc_ref, zs_ref):
    cs = cs_ref[...]
    for g in range(B_GROUPS):
        sl = slice(g * HEAD, (g + 1) * HEAD)
        y = _dot(z_ref[:, sl], cs) * (HEAD ** -0.5)
        zc_ref[:, sl] = y[:, :HEAD].astype(zc_ref.dtype)
        zs_ref[:, sl] = y[:, HEAD:].astype(zs_ref.dtype)


def chan_dft(nat, cs128, tr=512):
    t = nat.shape[0]
    tr = _pick(t, tr, 8)
    row = pl.BlockSpec((tr, B_W), lambda i: (i, 0))
    return pl.pallas_call(
        _chan_dft_kernel,
        grid=(t // tr,),
        in_specs=[row, pl.BlockSpec((HEAD, 2 * HEAD), lambda i: (0, 0))],
        out_specs=[row, row],
        out_shape=[jax.ShapeDtypeStruct((t, B_W), BF16)] * 2,
        compiler_params=_params(1),
        name="chan_dft",
    )(nat, cs128)


def _seq_dft_kernel(c_ref, s_ref, zc_ref, zs_ref, *rest, scale):
    o_ref = rest[-1]
    y = _dot(c_ref[...], zc_ref[...]) - _dot(s_ref[...], zs_ref[...])
    o_ref[...] = (y * scale).astype(o_ref.dtype)


def seq_dft(zc, zs, cos_t, sin_t, group, prev_out, tm=512, tn=512):
    row0, nseq, s = group
    t = zc.shape[0]
    tm = _pick(s, tm, 8)
    tn = _pick(B_W, tn)
    ni, nj = s // tm, B_W // tn
    sb0, rb0 = row0 // s, row0 // tm
    in_specs = [pl.BlockSpec((tm, s), lambda b, j, i: (i, 0)),
                pl.BlockSpec((tm, s), lambda b, j, i: (i, 0)),
                pl.BlockSpec((s, tn), lambda b, j, i: (sb0 + b, j)),
                pl.BlockSpec((s, tn), lambda b, j, i: (sb0 + b, j))]
    args = [cos_t, sin_t, zc, zs]
    aliases = {}
    if prev_out is not None:
        in_specs.append(pl.BlockSpec(memory_space=pl.ANY))
        args.append(prev_out)
        aliases = {4: 0}
    return pl.pallas_call(
        functools.partial(_seq_dft_kernel, scale=s ** -0.5),
        grid=(nseq, nj, ni),
        in_specs=in_specs,
        out_specs=pl.BlockSpec((tm, tn), lambda b, j, i: (rb0 + b * ni + i, j)),
        out_shape=jax.ShapeDtypeStruct((t, B_W), BF16),
        input_output_aliases=aliases,
        compiler_params=_params(3),
        name="seq_dft",
    )(*args)


def _dft_tables(n):
    k = lax.broadcasted_iota(jnp.int32, (n, n), 0)
    t = lax.broadcasted_iota(jnp.int32, (n, n), 1)
    ang = ((k * t) % n).astype(F32) * (2.0 * math.pi / n)
    return jnp.cos(ang).astype(BF16), jnp.sin(ang).astype(BF16)


def _dil_kernel(q_ref, kp_ref, km_ref, kn_ref, vp_ref, vm_ref, vn_ref, o_ref, l_ref, *,
                dil, half, rows, n_p, sub_p, sub_s, slopes):
    i = pl.program_id(1)
    start = i * rows
    in_p = start < n_p
    seq_len = jnp.where(in_p, sub_p, sub_s)
    local = jnp.where(in_p, start % sub_p, (start - n_p) % sub_s)
    qi = lax.broadcasted_iota(jnp.int32, (HEAD, 3 * HEAD), 0)
    ci = lax.broadcasted_iota(jnp.int32, (HEAD, 3 * HEAD), 1)
    rel = ci - HEAD - qi
    band = jnp.abs(rel) <= half
    dist = (jnp.abs(rel) * dil).astype(F32)
    nsub = rows // HEAD

    def key_block(main_ref, prev_ref, next_ref, j, sl):
        if j < 0:
            return prev_ref[:, sl]
        if j >= nsub:
            return next_ref[:, sl]
        return main_ref[j * HEAD:(j + 1) * HEAD, sl]

    for s in range(nsub):
        kpos = local + (s - 1) * HEAD + ci
        valid = band & (kpos >= 0) & (kpos < seq_len)
        for h in range(C_PER):
            sl = slice(h * HEAD, (h + 1) * HEAD)
            q = q_ref[s * HEAD:(s + 1) * HEAD, sl]
            sc = jnp.concatenate([_dot_nt(q, key_block(km_ref, kp_ref, kn_ref, s + j, sl))
                                  for j in (-1, 0, 1)], axis=-1)
            sc = jnp.where(valid, sc * SCALE - slopes[h] * dist, NEG)
            m = jnp.max(sc, axis=-1, keepdims=True)
            e = jnp.exp(sc - m)
            den = jnp.sum(e, axis=-1, keepdims=True)
            p = (e / den).astype(BF16)
            o = sum(_dot(p[:, (j + 1) * HEAD:(j + 2) * HEAD], key_block(vm_ref, vp_ref, vn_ref, s + j, sl))
                    for j in (-1, 0, 1))
            o_ref[s * HEAD:(s + 1) * HEAD, sl] = o
            l_ref[s * HEAD:(s + 1) * HEAD, sl] = jnp.broadcast_to(m + jnp.log(den), (HEAD, HEAD))


def dil_attn(qkv, col0, p_idx, groups, slopes, rows=512):
    window, dil = C_PATTERNS[p_idx]
    half = window // (2 * dil)
    assert half <= HEAD // 2
    (_, bp, sp), (_, bs, ss) = groups
    n = qkv.shape[1]
    n_p, sub_p, sub_s = bp * sp // dil, sp // dil, ss // dil
    assert sub_p % HEAD == 0 and sub_s % HEAD == 0
    rows = _pick(math.gcd(sub_p, sub_s), rows)
    nsub = rows // HEAD
    nhalo = n // HEAD

    main = lambda c: pl.BlockSpec((None, rows, C_OUT_W), lambda r, i: (r, i, col0 + c))
    prev = lambda c: pl.BlockSpec((None, HEAD, C_OUT_W),
                                  lambda r, i: (r, jnp.maximum(i * nsub - 1, 0), col0 + c))
    nxt = lambda c: pl.BlockSpec((None, HEAD, C_OUT_W),
                                 lambda r, i: (r, jnp.minimum((i + 1) * nsub, nhalo - 1), col0 + c))
    out_spec = pl.BlockSpec((None, rows, C_OUT_W), lambda r, i: (r, i, 0))
    out_sds = jax.ShapeDtypeStruct((dil, n, C_OUT_W), F32)
    return pl.pallas_call(
        functools.partial(_dil_kernel, dil=dil, half=half, rows=rows, n_p=n_p, sub_p=sub_p, sub_s=sub_s,
                          slopes=tuple(slopes[p_idx * C_PER:(p_idx + 1) * C_PER])),
        grid=(dil, n // rows),
        in_specs=[main(0), prev(1), main(1), nxt(1), prev(2), main(2), nxt(2)],
        out_specs=[out_spec, out_spec],
        out_shape=[out_sds, out_sds],
        compiler_params=_params(2),
        name=f"dil_attn_p{p_idx}",
    )(*([qkv] * 7))


def _dil_combine_kernel(*refs):
    srcs, out_ref, scr = refs[:2 * N_PAT], refs[2 * N_PAT], list(refs[2 * N_PAT + 1:])
    vals = []
    for src in srcs:
        dil = src.shape[0]
        if dil == 1:
            vals.append(src[0])
            continue
        nat = scr.pop(0)
        nchunk = nat.shape[0]
        for r in range(dil):
            for c in range(nchunk):
                nat[c, pl.ds(r, src.shape[1], stride=dil), :] = src[r, :, c * LANE:(c + 1) * LANE]
        vals.append(jnp.concatenate([nat[c] for c in range(nchunk)], axis=-1))
    outs, lses = vals[:N_PAT], vals[N_PAT:]
    m = functools.reduce(jnp.maximum, lses)
    es = [jnp.exp(l - m) for l in lses]
    num = sum(e * o for e, o in zip(es, outs))
    out_ref[...] = (num / sum(es)).astype(out_ref.dtype)


def dil_combine(outs, lses, tr=512):
    t = outs[0].shape[0] * outs[0].shape[1]
    tr = _pick(t, tr, 8 * max(d for _, d in C_PATTERNS))
    spec = lambda a: pl.BlockSpec((a.shape[0], tr // a.shape[0], C_OUT_W), lambda i: (0, i, 0))
    n_scr = sum(2 for a in outs if a.shape[0] > 1)
    return pl.pallas_call(
        _dil_combine_kernel,
        grid=(t // tr,),
        in_specs=[spec(a) for a in outs + lses],
        out_specs=pl.BlockSpec((tr, C_OUT_W), lambda i: (i, 0)),
        out_shape=jax.ShapeDtypeStruct((t, C_OUT_W), BF16),
        scratch_shapes=[pltpu.VMEM((C_OUT_W // LANE, tr, LANE), F32)] * n_scr,
        compiler_params=_params(1),
        name="dil_combine",
    )(*outs, *lses)


def _xattn_kernel(q_ref, kv_ref, o_ref):
    for h in range(X_HEADS):
        sl = slice(h * HEAD, (h + 1) * HEAD)
        s = _dot_nt(q_ref[:, sl], kv_ref[:, sl]) * SCALE
        p = jnp.exp(s - jnp.max(s, axis=-1, keepdims=True))
        l = jnp.sum(p, axis=-1, keepdims=True)
        o = _dot(p.astype(BF16), kv_ref[:, X_W + h * HEAD:X_W + (h + 1) * HEAD]) / l
        o_ref[:, sl] = o.astype(o_ref.dtype)


def xattn(q, kv, groups, n_mem, tq=512):
    t = q.shape[0]
    (_, bp, sp), (_, _, ss) = groups
    tq = _pick(math.gcd(sp, ss), tq, 8)
    n_p = bp * sp // tq

    def seq_of(i):
        return jnp.where(i < n_p, i // (sp // tq), bp + (i - n_p) // (ss // tq))

    return pl.pallas_call(
        _xattn_kernel,
        grid=(t // tq,),
        in_specs=[pl.BlockSpec((tq, X_W), lambda i: (i, 0)),
                  pl.BlockSpec((n_mem, 2 * X_W), lambda i: (seq_of(i), 0))],
        out_specs=pl.BlockSpec((tq, X_W), lambda i: (i, 0)),
        out_shape=jax.ShapeDtypeStruct((t, X_W), BF16),
        compiler_params=_params(1),
        name="xattn",
    )(q, kv)


def _rope_tables(seq_len):
    t = jnp.arange(seq_len)
    row = (t // GRID_W).astype(F32)
    col = (t % GRID_W).astype(F32)
    n_pairs = HEAD // 4
    inv_freq = ROPE_BASE ** (-jnp.arange(n_pairs, dtype=F32) / n_pairs)
    ang = jnp.concatenate([row[:, None] * inv_freq[None, :], col[:, None] * inv_freq[None, :]], axis=-1)
    cos = jnp.repeat(jnp.cos(ang), 2, axis=-1)
    sin = jnp.repeat(jnp.sin(ang), 2, axis=-1)
    even = (jnp.arange(HEAD) % 2 == 0)[None, :]
    return cos, jnp.where(even, -sin, 0.0), jnp.where(even, 0.0, sin)


def _alibi_slopes():
    h = np.arange(1, C_HEADS + 1, dtype=np.float32)
    return [float(v) for v in np.float32(2.0) ** (-np.float32(ALIBI_MAX_EXP) * h / np.float32(C_HEADS))]


def kernel(x_prompt, x_sample, mem_prompt, mem_sample, ffn1_pre_norm, ffn1_w_in, ffn1_w_out, ffn1_post_norm, mix_pre_norm, w_mix_in, a_q_norm, a_k_norm, w_a_proj, w_b_proj, w_c_proj, w_branch_gate, b_branch_gate, w_mix_out, mix_post_norm, xattn_pre_norm, mem_norm, w_xq, w_xkv, w_xo, xattn_post_norm, ffn2_pre_norm, ffn2_w_in, ffn2_w_out, ffn2_post_norm):
    bp, sp, d = x_prompt.shape
    bs, ss, _ = x_sample.shape
    n_mem = mem_prompt.shape[1]
    depth = ffn1_w_in.shape[0]
    tp = bp * sp
    groups = ((0, bp, sp), (tp, bs, ss))
    assert tp % ss == 0 and sp % GRID_W == 0 and ss % GRID_W == 0

    x = jnp.concatenate([x_prompt.reshape(tp, d), x_sample.reshape(bs * ss, d)], axis=0)
    mem = jnp.concatenate([mem_prompt.reshape(bp * n_mem, d), mem_sample.reshape(bs * n_mem, d)], axis=0)

    rope = _rope_tables(max(sp, ss))
    dft = {s: _dft_tables(s) for s in {sp, ss}}
    kc = np.arange(HEAD)
    ang128 = 2.0 * np.pi * ((kc[:, None] * kc[None, :]) % HEAD) / HEAD
    cs128 = jnp.asarray(np.concatenate([np.cos(ang128), np.sin(ang128)], axis=1), BF16)
    slopes = _alibi_slopes()
    head_scale = jnp.concatenate([jnp.full((A_Q_HEADS, HEAD), SCALE, F32),
                                  jnp.ones((A_KV_HEADS, HEAD), F32)], axis=0)
    qk_gains = jnp.concatenate([jnp.repeat(a_q_norm[:, None, :], A_Q_HEADS, axis=1),
                                jnp.repeat(a_k_norm[:, None, :], A_KV_HEADS, axis=1)], axis=1)

    bf = lambda w: w.astype(BF16)
    ffn1_w_in, ffn1_w_out, ffn2_w_in, ffn2_w_out = bf(ffn1_w_in), bf(ffn1_w_out), bf(ffn2_w_in), bf(ffn2_w_out)
    w_mix_in, w_branch_gate, w_mix_out = bf(w_mix_in), bf(w_branch_gate), bf(w_mix_out)
    w_a_proj, w_b_proj, w_c_proj = bf(w_a_proj), bf(w_b_proj), bf(w_c_proj)
    w_xq, w_xkv, w_xo = bf(w_xq), bf(w_xkv), bf(w_xo)

    nat_block = lambda j: jnp.where(j < NAT_V, BLK_B + j,
                                    jnp.where(j == NAT_V, BLK_V, BLK_CQ + N_PAT * (j - NAT_C)))

    h = rms_rows(x, ffn1_pre_norm, 0)
    for l in range(depth):
        act = swiglu_in(h, ffn1_w_in, l)
        o = matmul(act, ffn1_w_out, l, F32, tm=512, tn=512, name="ffn_out")
        x, u = resid_norm(o, x, ffn1_post_norm, l, mix_pre_norm, l, 0.5)

        qk = mix_qk(u, w_mix_in, l, qk_gains, head_scale, rope, groups)
        nat = matmul(u, w_mix_in, l, BF16, tn=C_OUT_W, n_out=NAT_W, col_block=nat_block, name="mix_nat")
        y_a = y_b = None
        for g in groups:
            y_a = attn_a(qk, nat, g, y_a)
        zc, zs = chan_dft(nat, cs128)
        for g in groups:
            y_b = seq_dft(zc, zs, *dft[g[2]], g, y_b)
        outs, lses = [], []
        for p_idx, (_, dil) in enumerate(C_PATTERNS):
            if dil == 1:
                qkv, col0 = nat.reshape(1, *nat.shape), NAT_C
            else:
                qkv, col0 = mix_dil(u, w_mix_in, l, p_idx), 0
            o_p, l_p = dil_attn(qkv, col0, p_idx, groups, slopes)
            outs.append(o_p)
            lses.append(l_p)
        y_c = dil_combine(outs, lses)
        merged = merge_branches(u, y_a, y_b, y_c, w_branch_gate, b_branch_gate,
                                w_a_proj, w_b_proj, w_c_proj, l)
        o = matmul(merged, w_mix_out, l, F32, name="mix_out")
        x, u = resid_norm(o, x, mix_post_norm, l, xattn_pre_norm, l, 1.0)

        q = matmul(u, w_xq, l, BF16, name="xattn_q")
        kv = matmul(rms_rows(mem, mem_norm, l), w_xkv, l, BF16, name="xattn_kv")
        x, h = out_proj_resid(xattn(q, kv, groups, n_mem), w_xo, x, xattn_post_norm, ffn2_pre_norm, l)

        act = swiglu_in(h, ffn2_w_in, l)
        o = matmul(act, ffn2_w_out, l, F32, tm=512, tn=512, name="ffn_out")
        last = l + 1 == depth
        x, h = resid_norm(o, x, ffn2_post_norm, l, None if last else ffn1_pre_norm, l + 1, 0.5)

    return x[:tp].reshape(bp, sp, d), x[tp:].reshape(bs, ss, d)
```

```python
import functools
import math

import numpy as np
import jax
import jax.numpy as jnp
from jax import lax
from jax.experimental import pallas as pl
from jax.experimental.pallas import tpu as pltpu

F32 = jnp.float32
BF16 = jnp.bfloat16

HEAD = 128
GRID_W = 64
EPS = 1e-6
ROPE_BASE = 10000.0
A_Q_HEADS = 12
A_KV_HEADS = 4
A_GROUPS = A_Q_HEADS // A_KV_HEADS
B_GROUPS = 20
C_PATTERNS = ((128, 1), (512, 4), (2048, 16))
C_PER = 4
C_HEADS = C_PER * len(C_PATTERNS)
ALIBI_MAX_EXP = 8.0
X_HEADS = 4
N_BRANCH = 3

A_Q_W = A_Q_HEADS * HEAD
A_KV_W = A_KV_HEADS * HEAD
QK_W = A_Q_W + A_KV_W
B_W = B_GROUPS * HEAD
C_W = C_HEADS * HEAD
C_OUT_W = C_PER * HEAD
X_W = X_HEADS * HEAD
MIX_W = A_Q_W + 2 * A_KV_W + B_W + 3 * C_W
BLK_V = QK_W // C_OUT_W
BLK_CQ = (QK_W + A_KV_W + B_W) // C_OUT_W
N_PAT = len(C_PATTERNS)
BLK_B = (QK_W + A_KV_W) // C_OUT_W
NAT_W = B_W + A_KV_W + 3 * C_OUT_W
NAT_V = B_W // C_OUT_W
NAT_C = NAT_V + 1
SCALE = HEAD ** -0.5
NEG = -1e30
HALO = HEAD // 2

VMEM_LIMIT_V7X = 56 * 1024 * 1024
LANE = 128


def _params(n_axes):
    return pltpu.CompilerParams(dimension_semantics=("arbitrary",) * n_axes,
                                vmem_limit_bytes=VMEM_LIMIT_V7X)


def _pick(n, pref, unit=LANE):
    if n <= pref:
        return n
    best = None
    for t in range(unit, pref + 1, unit):
        if n % t == 0:
            best = t
    assert best is not None, (n, pref)
    return best


def _wspec(layer, shape, imap):
    return pl.BlockSpec((None,) + shape, lambda *g: (layer,) + tuple(imap(*g)))


def _rms(x, g):
    return x * lax.rsqrt(jnp.mean(x * x, axis=-1, keepdims=True) + EPS) * g


def _sigmoid(x):
    return 1.0 / (1.0 + jnp.exp(-x))


def _dot(a, b):
    return jnp.dot(a, b, preferred_element_type=F32)


def _dot_nt(a, b):
    return lax.dot_general(a, b, (((1,), (1,)), ((), ())), preferred_element_type=F32)


def _row_inputs(x, tr):
    if not isinstance(x, tuple):
        return [x], [pl.BlockSpec((tr, x.shape[1]), lambda i: (i, 0))], None
    xp, xs = x
    n_p = xp.shape[0] // tr
    d = xp.shape[1]
    return ([xp, xs],
            [pl.BlockSpec((tr, d), lambda i: (jnp.minimum(i, n_p - 1), 0)),
             pl.BlockSpec((tr, d), lambda i: (jnp.maximum(i - n_p, 0), 0))], n_p)


def _load_rows(x_refs, n_p):
    if len(x_refs) == 1:
        return x_refs[0][...]
    return jnp.where(pl.program_id(0) < n_p, x_refs[0][...], x_refs[1][...])


def _rms_kernel(*refs, n_p):
    x_refs, g_ref, o_ref = refs[:-2], refs[-2], refs[-1]
    o_ref[...] = _rms(_load_rows(x_refs, n_p), g_ref[...]).astype(o_ref.dtype)


def rms_rows(x, g, layer, tr=256):
    pieces = x if isinstance(x, tuple) else (x,)
    t = sum(a.shape[0] for a in pieces)
    d = g.shape[-1]
    tr = _pick(math.gcd(*[a.shape[0] for a in pieces]), tr, 8)
    arrs, specs, n_p = _row_inputs(x, tr)
    return pl.pallas_call(
        functools.partial(_rms_kernel, n_p=n_p),
        grid=(t // tr,),
        in_specs=specs + [_wspec(layer, (1, d), lambda i: (0, 0))],
        out_specs=pl.BlockSpec((tr, d), lambda i: (i, 0)),
        out_shape=jax.ShapeDtypeStruct((t, d), BF16),
        compiler_params=_params(1),
        name="rms_rows",
    )(*arrs, g.reshape(g.shape[0], 1, d))


def _resid_kernel(*refs, coef, n_x, n_p, has_next):
    o_ref, x_refs, gp_ref, rest = refs[0], refs[1:1 + n_x], refs[1 + n_x], refs[2 + n_x:]
    xn = _load_rows(x_refs, n_p) + coef * _rms(o_ref[...].astype(F32), gp_ref[...])
    if has_next:
        gn_ref, xo_ref, h_ref = rest
        xo_ref[...] = xn
        h_ref[...] = _rms(xn, gn_ref[...]).astype(h_ref.dtype)
    else:
        yp_ref, ys_ref = rest
        i = pl.program_id(0)

        @pl.when(i < n_p)
        def _():
            yp_ref[...] = xn

        @pl.when(i >= n_p)
        def _():
            ys_ref[...] = xn


def resid_norm(o, x, g_post, layer, g_next, next_layer, coef, rows_p, tr=256):
    t, d = o.shape
    tr = _pick(math.gcd(rows_p, t - rows_p), tr, 8)
    arrs, specs, n_p = _row_inputs(x, tr)
    row = pl.BlockSpec((tr, d), lambda i: (i, 0))
    vec = lambda lyr: _wspec(lyr, (1, d), lambda i: (0, 0))
    g3 = lambda g: g.reshape(g.shape[0], 1, d)
    if g_next is None:
        n_p = rows_p // tr
        return pl.pallas_call(
            functools.partial(_resid_kernel, coef=coef, n_x=len(arrs), n_p=n_p, has_next=False),
            grid=(t // tr,),
            in_specs=[row] + specs + [vec(layer)],
            out_specs=[pl.BlockSpec((tr, d), lambda i: (jnp.minimum(i, n_p - 1), 0)),
                       pl.BlockSpec((tr, d), lambda i: (jnp.maximum(i - n_p, 0), 0))],
            out_shape=[jax.ShapeDtypeStruct((rows_p, d), F32), jax.ShapeDtypeStruct((t - rows_p, d), F32)],
            compiler_params=_params(1),
            name="resid_last",
        )(o, *arrs, g3(g_post))
    return pl.pallas_call(
        functools.partial(_resid_kernel, coef=coef, n_x=len(arrs), n_p=n_p, has_next=True),
        grid=(t // tr,),
        in_specs=[row] + specs + [vec(layer), vec(next_layer)],
        out_specs=[row, row],
        out_shape=[jax.ShapeDtypeStruct((t, d), F32), jax.ShapeDtypeStruct((t, d), BF16)],
        compiler_params=_params(1),
        name="resid_norm",
    )(o, *arrs, g3(g_post), g3(g_next))


def _mm_kernel(x_ref, w_ref, o_ref):
    o_ref[...] = _dot(x_ref[...], w_ref[...]).astype(o_ref.dtype)


def matmul(x, w, layer, out_dtype, tm=1024, tn=1024, n_out=None, col_block=None, name="matmul"):
    m, k = x.shape
    n = w.shape[2] if n_out is None else n_out
    tm = _pick(m, tm, 8)
    tn = _pick(n, tn)
    col_block = col_block or (lambda j: j)
    return pl.pallas_call(
        _mm_kernel,
        grid=(m // tm, n // tn),
        in_specs=[pl.BlockSpec((tm, k), lambda i, j: (i, 0)),
                  _wspec(layer, (k, tn), lambda i, j: (0, col_block(j)))],
        out_specs=pl.BlockSpec((tm, tn), lambda i, j: (i, j)),
        out_shape=jax.ShapeDtypeStruct((m, n), out_dtype),
        compiler_params=_params(2),
        name=name,
    )(x, w)


def _swiglu_kernel(x_ref, wg_ref, wu_ref, o_ref):
    x = x_ref[...]
    g = _dot(x, wg_ref[...])
    u = _dot(x, wu_ref[...])
    o_ref[...] = (g * _sigmoid(g) * u).astype(o_ref.dtype)


def swiglu_in(h, w_in, layer, tm=2048, tn=256):
    t, d = h.shape
    f = w_in.shape[2] // 2
    tm = _pick(t, tm, 8)
    tn = _pick(f, tn)
    nj = f // tn
    return pl.pallas_call(
        _swiglu_kernel,
        grid=(t // tm, nj),
        in_specs=[pl.BlockSpec((tm, d), lambda i, j: (i, 0)),
                  _wspec(layer, (d, tn), lambda i, j: (0, j)),
                  _wspec(layer, (d, tn), lambda i, j: (0, nj + j))],
        out_specs=pl.BlockSpec((tm, tn), lambda i, j: (i, j)),
        out_shape=jax.ShapeDtypeStruct((t, f), BF16),
        compiler_params=_params(2),
        name="swiglu_in",
    )(h, w_in, w_in)


def _merge_kernel(u_ref, a_ref, b_ref, c_ref, wg0_ref, wg1_ref, wg2_ref, bg0_ref, bg1_ref, bg2_ref,
                  wa_ref, wb_ref, wc_ref, o_ref):
    u = u_ref[...]
    acc = _sigmoid(_dot(u, wg0_ref[...]) + bg0_ref[...]) * _dot(a_ref[...], wa_ref[...])
    acc += _sigmoid(_dot(u, wg1_ref[...]) + bg1_ref[...]) * _dot(b_ref[...], wb_ref[...])
    acc += _sigmoid(_dot(u, wg2_ref[...]) + bg2_ref[...]) * _dot(c_ref[...], wc_ref[...])
    o_ref[...] = acc.astype(o_ref.dtype)


def merge_branches(u, a, b, c, w_gate, b_gate, w_a, w_b, w_c, layer, tm=512, tn=256):
    t, d = u.shape
    tm = _pick(t, tm, 8)
    tn = _pick(d, tn)
    nj = d // tn
    act = lambda width: pl.BlockSpec((tm, width), lambda i, j: (i, 0))
    gate_w = lambda br: _wspec(layer, (d, tn), lambda i, j: (0, br * nj + j))
    gate_b = lambda br: _wspec(layer, (1, tn), lambda i, j: (0, br * nj + j))
    proj_w = lambda width: _wspec(layer, (width, tn), lambda i, j: (0, j))
    bg = b_gate.reshape(b_gate.shape[0], 1, N_BRANCH * d)
    return pl.pallas_call(
        _merge_kernel,
        grid=(t // tm, nj),
        in_specs=[act(d), act(A_Q_W), act(B_W), act(C_OUT_W),
                  gate_w(0), gate_w(1), gate_w(2), gate_b(0), gate_b(1), gate_b(2),
                  proj_w(A_Q_W), proj_w(B_W), proj_w(C_OUT_W)],
        out_specs=pl.BlockSpec((tm, tn), lambda i, j: (i, j)),
        out_shape=jax.ShapeDtypeStruct((t, d), BF16),
        compiler_params=_params(2),
        name="merge_branches",
    )(u, a, b, c, w_gate, w_gate, w_gate, bg, bg, bg, w_a, w_b, w_c)


def _out_resid_kernel(a_ref, w_ref, x_ref, gp_ref, gn_ref, xo_ref, h_ref):
    xn = x_ref[...] + _rms(_dot(a_ref[...], w_ref[...]), gp_ref[...])
    xo_ref[...] = xn
    h_ref[...] = _rms(xn, gn_ref[...]).astype(h_ref.dtype)


def out_proj_resid(a, w, x, g_post, g_next, layer, tm=256):
    t, d = x.shape
    k = a.shape[1]
    tm = _pick(t, tm, 8)
    row = pl.BlockSpec((tm, d), lambda i: (i, 0))
    vec = _wspec(layer, (1, d), lambda i: (0, 0))
    g3 = lambda g: g.reshape(g.shape[0], 1, d)
    return pl.pallas_call(
        _out_resid_kernel,
        grid=(t // tm,),
        in_specs=[pl.BlockSpec((tm, k), lambda i: (i, 0)),
                  _wspec(layer, (k, d), lambda i: (0, 0)), row, vec, vec],
        out_specs=[row, row],
        out_shape=[jax.ShapeDtypeStruct((t, d), F32), jax.ShapeDtypeStruct((t, d), BF16)],
        compiler_params=_params(1),
        name="out_proj_resid",
    )(a, w, x, g3(g_post), g3(g_next))


def _mix_qk_kernel(u_ref, w_ref, g_ref, sc_ref, cs_ref, sa_ref, sb_ref, o_ref):
    y = _dot(u_ref[...], w_ref[...])
    cs, sa, sb = cs_ref[...], sa_ref[...], sb_ref[...]
    for h in range(o_ref.shape[1] // HEAD):
        sl = slice(h * HEAD, (h + 1) * HEAD)
        x = _rms(y[:, sl], g_ref[h:h + 1, :])
        r = x * cs + pltpu.roll(x, HEAD - 1, 1) * sa + pltpu.roll(x, 1, 1) * sb
        o_ref[:, sl] = (r * sc_ref[h:h + 1, :]).astype(o_ref.dtype)


def mix_qk(u, w_mix_in, layer, gains, scales, rope, groups, tm=1024, tn=1024):
    t, d = u.shape
    (_, bp, sp), (_, _, ss) = groups
    tm = _pick(math.gcd(sp, ss), tm, 8)
    tn = _pick(QK_W, tn)
    n_p = bp * sp // tm
    hpt = tn // HEAD

    def pos_block(i):
        return jnp.where(i < n_p, i % (sp // tm), (i - n_p) % (ss // tm))

    tab = pl.BlockSpec((tm, HEAD), lambda i, j: (pos_block(i), 0))
    return pl.pallas_call(
        _mix_qk_kernel,
        grid=(t // tm, QK_W // tn),
        in_specs=[pl.BlockSpec((tm, d), lambda i, j: (i, 0)),
                  _wspec(layer, (d, tn), lambda i, j: (0, j)),
                  _wspec(layer, (hpt, HEAD), lambda i, j: (j, 0)),
                  pl.BlockSpec((hpt, HEAD), lambda i, j: (j, 0)),
                  tab, tab, tab],
        out_specs=pl.BlockSpec((tm, tn), lambda i, j: (i, j)),
        out_shape=jax.ShapeDtypeStruct((t, QK_W), BF16),
        compiler_params=_params(2),
        name="mix_qk",
    )(u, w_mix_in, gains, scales, *rope)


def _mix_dil_kernel(u_ref, w_ref, o_ref, scr_ref, *, dil):
    y = _dot(u_ref[...], w_ref[...])
    nchunk, tm, _ = scr_ref.shape
    for c in range(nchunk):
        scr_ref[c] = y[:, c * LANE:(c + 1) * LANE]
    for r in range(dil):
        for c in range(nchunk):
            o_ref[r, :, c * LANE:(c + 1) * LANE] = (
                scr_ref[c, pl.ds(r, tm // dil, stride=dil), :].astype(o_ref.dtype))


def mix_dil(u, w_mix_in, layer, p_idx, tm=1024):
    t, d = u.shape
    dil = C_PATTERNS[p_idx][1]
    tm = _pick(t, tm, 8 * dil)
    return pl.pallas_call(
        functools.partial(_mix_dil_kernel, dil=dil),
        grid=(t // tm, 3),
        in_specs=[pl.BlockSpec((tm, d), lambda i, j: (i, 0)),
                  _wspec(layer, (d, C_OUT_W), lambda i, j: (0, BLK_CQ + p_idx + N_PAT * j))],
        out_specs=pl.BlockSpec((dil, tm // dil, C_OUT_W), lambda i, j: (0, i, j)),
        out_shape=jax.ShapeDtypeStruct((dil, t // dil, 3 * C_OUT_W), BF16),
        scratch_shapes=[pltpu.VMEM((C_OUT_W // LANE, tm, LANE), F32)],
        compiler_params=_params(2),
        name=f"mix_dil_p{p_idx}",
    )(u, w_mix_in)


def _attn_a_kernel(q_ref, k_ref, v_ref, *rest):
    o_ref = rest[-1]
    k = k_ref[...]
    v = v_ref[...]
    for g in range(A_GROUPS):
        sl = slice(g * HEAD, (g + 1) * HEAD)
        s = _dot_nt(q_ref[:, sl], k)
        p = jnp.exp(s - jnp.max(s, axis=-1, keepdims=True))
        l = jnp.sum(p, axis=-1, keepdims=True)
        o = _dot(p.astype(BF16), v) * (1.0 / l)
        o_ref[:, sl] = o.astype(o_ref.dtype)


def attn_a(qk, nat, group, prev_out, tq=256):
    row0, nseq, s = group
    t = qk.shape[0]
    tq = _pick(s, tq, 8)
    nq = s // tq
    rb0, sb0 = row0 // tq, row0 // s
    gw = A_GROUPS * HEAD
    in_specs = [pl.BlockSpec((tq, gw), lambda b, h, i: (rb0 + b * nq + i, h)),
                pl.BlockSpec((s, HEAD), lambda b, h, i: (sb0 + b, A_Q_HEADS + h)),
                pl.BlockSpec((s, HEAD), lambda b, h, i: (sb0 + b, NAT_V * (C_OUT_W // HEAD) + h))]
    args = [qk, qk, nat]
    aliases = {}
    if prev_out is not None:
        in_specs.append(pl.BlockSpec(memory_space=pl.ANY))
        args.append(prev_out)
        aliases = {3: 0}
    return pl.pallas_call(
        _attn_a_kernel,
        grid=(nseq, A_KV_HEADS, nq),
        in_specs=in_specs,
        out_specs=pl.BlockSpec((tq, gw), lambda b, h, i: (rb0 + b * nq + i, h)),
        out_shape=jax.ShapeDtypeStruct((t, A_Q_W), BF16),
        input_output_aliases=aliases,
        compiler_params=_params(3),
        name="attn_a",
    )(*args)


def _chan_dft_kernel(z_ref, cs_ref, zc_ref, zs_ref):
    cs = cs_ref[...]
    for g in range(B_GROUPS):
        sl = slice(g * HEAD, (g + 1) * HEAD)
        y = _dot(z_ref[:, sl], cs) * (HEAD ** -0.5)
        zc_ref[:, sl] = y[:, :HEAD].astype(zc_ref.dtype)
        zs_ref[:, sl] = y[:, HEAD:].astype(zs_ref.dtype)


def chan_dft(nat, cs128, tr=512):
    t = nat.shape[0]
    tr = _pick(t, tr, 8)
    row = pl.BlockSpec((tr, B_W), lambda i: (i, 0))
    return pl.pallas_call(
        _chan_dft_kernel,
        grid=(t // tr,),
        in_specs=[row, pl.BlockSpec((HEAD, 2 * HEAD), lambda i: (0, 0))],
        out_specs=[row, row],
        out_shape=[jax.ShapeDtypeStruct((t, B_W), BF16)] * 2,
        compiler_params=_params(1),
        name="chan_dft",
    )(nat, cs128)


def _seq_dft_kernel(c_ref, s_ref, zc_ref, zs_ref, *rest, scale):
    o_ref = rest[-1]
    y = _dot(c_ref[...], zc_ref[...]) - _dot(s_ref[...], zs_ref[...])
    o_ref[...] = (y * scale).astype(o_ref.dtype)


def seq_dft(zc, zs, cos_t, sin_t, group, prev_out, tm=512, tn=512):
    row0, nseq, s = group
    t = zc.shape[0]
    tm = _pick(s, tm, 8)
    tn = _pick(B_W, tn)
    ni, nj = s // tm, B_W // tn
    sb0, rb0 = row0 // s, row0 // tm
    in_specs = [pl.BlockSpec((tm, s), lambda b, j, i: (i, 0)),
                pl.BlockSpec((tm, s), lambda b, j, i: (i, 0)),
                pl.BlockSpec((s, tn), lambda b, j, i: (sb0 + b, j)),
                pl.BlockSpec((s, tn), lambda b, j, i: (sb0 + b, j))]
    args = [cos_t, sin_t, zc, zs]
    aliases = {}
    if prev_out is not None:
        in_specs.append(pl.BlockSpec(memory_space=pl.ANY))
        args.append(prev_out)
        aliases = {4: 0}
    return pl.pallas_call(
        functools.partial(_seq_dft_kernel, scale=s ** -0.5),
        grid=(nseq, nj, ni),
        in_specs=in_specs,
        out_specs=pl.BlockSpec((tm, tn), lambda b, j, i: (rb0 + b * ni + i, j)),
        out_shape=jax.ShapeDtypeStruct((t, B_W), BF16),
        input_output_aliases=aliases,
        compiler_params=_params(3),
        name="seq_dft",
    )(*args)


def _dft_tables(n):
    k = lax.broadcasted_iota(jnp.int32, (n, n), 0)
    t = lax.broadcasted_iota(jnp.int32, (n, n), 1)
    ang = ((k * t) % n).astype(F32) * (2.0 * math.pi / n)
    return jnp.cos(ang).astype(BF16), jnp.sin(ang).astype(BF16)


def _dil_kernel(q_ref, kp_ref, km_ref, kn_ref, vp_ref, vm_ref, vn_ref, o_ref, l_ref, *,
                dil, half, rows, n_p, sub_p, sub_s, slopes):
    i = pl.program_id(1)
    start = i * rows
    in_p = start < n_p
    seq_len = jnp.where(in_p, sub_p, sub_s)
    local = jnp.where(in_p, start % sub_p, (start - n_p) % sub_s)
    qi = lax.broadcasted_iota(jnp.int32, (HEAD, 2 * HEAD), 0)
    ci = lax.broadcasted_iota(jnp.int32, (HEAD, 2 * HEAD), 1)
    rel = ci - HALO - qi
    band = jnp.abs(rel) <= half
    dist = (jnp.abs(rel) * dil).astype(F32)
    nsub = rows // HEAD

    def window(main_ref, prev_ref, next_ref, r, s, sl):
        lo, hi = s * HEAD - HALO, (s + 1) * HEAD + HALO
        parts = [prev_ref[r, :, sl]] if lo < 0 else []
        parts.append(main_ref[r, max(lo, 0):min(hi, rows), sl])
        if hi > rows:
            parts.append(next_ref[r, :, sl])
        return parts[0] if len(parts) == 1 else jnp.concatenate(parts, axis=0)

    for s in range(nsub):
        kpos = local + s * HEAD - HALO + ci
        valid = band & (kpos >= 0) & (kpos < seq_len)
        qs = slice(s * HEAD, (s + 1) * HEAD)
        for h in range(C_PER):
            sl = slice(h * HEAD, (h + 1) * HEAD)
            bias = jnp.where(valid, -slopes[h] * dist, NEG)
            for r in range(q_ref.shape[0]):
                sc = _dot_nt(q_ref[r, qs, sl], window(km_ref, kp_ref, kn_ref, r, s, sl))
                sc = sc * SCALE + bias
                m = jnp.max(sc, axis=-1, keepdims=True)
                e = jnp.exp(sc - m)
                den = jnp.sum(e, axis=-1, keepdims=True)
                p = (e * (1.0 / den)).astype(BF16)
                o_ref[r, qs, sl] = _dot(p, window(vm_ref, vp_ref, vn_ref, r, s, sl))
                l_ref[r, qs, sl] = jnp.broadcast_to(m + jnp.log(den), (HEAD, HEAD))


def dil_attn(qkv, col0, p_idx, groups, slopes, rows_per_step=512):
    window, dil = C_PATTERNS[p_idx]
    half = window // (2 * dil)
    assert half <= HALO
    (_, bp, sp), (_, bs, ss) = groups
    n = qkv.shape[1]
    n_p, sub_p, sub_s = bp * sp // dil, sp // dil, ss // dil
    assert sub_p % HEAD == 0 and sub_s % HEAD == 0
    rows = _pick(math.gcd(sub_p, sub_s), rows_per_step)
    per = rows // HALO
    nhalo = n // HALO
    nres = math.gcd(dil, max(1, rows_per_step // rows))

    main = lambda c: pl.BlockSpec((nres, rows, C_OUT_W), lambda r, i: (r, i, col0 + c))
    prev = lambda c: pl.BlockSpec((nres, HALO, C_OUT_W),
                                  lambda r, i: (r, jnp.maximum(i * per - 1, 0), col0 + c))
    nxt = lambda c: pl.BlockSpec((nres, HALO, C_OUT_W),
                                 lambda r, i: (r, jnp.minimum((i + 1) * per, nhalo - 1), col0 + c))
    out_spec = pl.BlockSpec((nres, rows, C_OUT_W), lambda r, i: (r, i, 0))
    out_sds = jax.ShapeDtypeStruct((dil, n, C_OUT_W), F32)
    return pl.pallas_call(
        functools.partial(_dil_kernel, dil=dil, half=half, rows=rows, n_p=n_p, sub_p=sub_p, sub_s=sub_s,
                          slopes=tuple(slopes[p_idx * C_PER:(p_idx + 1) * C_PER])),
        grid=(dil // nres, n // rows),
        in_specs=[main(0), prev(1), main(1), nxt(1), prev(2), main(2), nxt(2)],
        out_specs=[out_spec, out_spec],
        out_shape=[out_sds, out_sds],
        compiler_params=_params(2),
        name=f"dil_attn_p{p_idx}",
    )(*([qkv] * 7))


def _dil_combine_kernel(*refs):
    srcs, out_ref, scr = refs[:2 * N_PAT], refs[2 * N_PAT], list(refs[2 * N_PAT + 1:])
    vals = []
    for src in srcs:
        dil = src.shape[0]
        if dil == 1:
            vals.append(src[0])
            continue
        nat = scr.pop(0)
        nchunk = nat.shape[0]
        for r in range(dil):
            for c in range(nchunk):
                nat[c, pl.ds(r, src.shape[1], stride=dil), :] = src[r, :, c * LANE:(c + 1) * LANE]
        vals.append(jnp.concatenate([nat[c] for c in range(nchunk)], axis=-1))
    outs, lses = vals[:N_PAT], vals[N_PAT:]
    m = functools.reduce(jnp.maximum, lses)
    es = [jnp.exp(l - m) for l in lses]
    num = sum(e * o for e, o in zip(es, outs))
    out_ref[...] = (num / sum(es)).astype(out_ref.dtype)


def dil_combine(outs, lses, tr=512):
    t = outs[0].shape[0] * outs[0].shape[1]
    tr = _pick(t, tr, 8 * max(d for _, d in C_PATTERNS))
    spec = lambda a: pl.BlockSpec((a.shape[0], tr // a.shape[0], C_OUT_W), lambda i: (0, i, 0))
    n_scr = sum(2 for a in outs if a.shape[0] > 1)
    return pl.pallas_call(
        _dil_combine_kernel,
        grid=(t // tr,),
        in_specs=[spec(a) for a in outs + lses],
        out_specs=pl.BlockSpec((tr, C_OUT_W), lambda i: (i, 0)),
        out_shape=jax.ShapeDtypeStruct((t, C_OUT_W), BF16),
        scratch_shapes=[pltpu.VMEM((C_OUT_W // LANE, tr, LANE), F32)] * n_scr,
        compiler_params=_params(1),
        name="dil_combine",
    )(*outs, *lses)


def _xattn_kernel(q_ref, kv_ref, o_ref):
    for h in range(X_HEADS):
        sl = slice(h * HEAD, (h + 1) * HEAD)
        s = _dot_nt(q_ref[:, sl], kv_ref[:, sl]) * SCALE
        p = jnp.exp(s - jnp.max(s, axis=-1, keepdims=True))
        l = jnp.sum(p, axis=-1, keepdims=True)
        o = _dot(p.astype(BF16), kv_ref[:, X_W + h * HEAD:X_W + (h + 1) * HEAD]) / l
        o_ref[:, sl] = o.astype(o_ref.dtype)


def xattn(q, kv, groups, n_mem, tq=512):
    t = q.shape[0]
    (_, bp, sp), (_, _, ss) = groups
    tq = _pick(math.gcd(sp, ss), tq, 8)
    n_p = bp * sp // tq

    def seq_of(i):
        return jnp.where(i < n_p, i // (sp // tq), bp + (i - n_p) // (ss // tq))

    return pl.pallas_call(
        _xattn_kernel,
        grid=(t // tq,),
        in_specs=[pl.BlockSpec((tq, X_W), lambda i: (i, 0)),
                  pl.BlockSpec((n_mem, 2 * X_W), lambda i: (seq_of(i), 0))],
        out_specs=pl.BlockSpec((tq, X_W), lambda i: (i, 0)),
        out_shape=jax.ShapeDtypeStruct((t, X_W), BF16),
        compiler_params=_params(1),
        name="xattn",
    )(q, kv)


def _rope_tables(seq_len):
    t = jnp.arange(seq_len)
    row = (t // GRID_W).astype(F32)
    col = (t % GRID_W).astype(F32)
    n_pairs = HEAD // 4
    inv_freq = ROPE_BASE ** (-jnp.arange(n_pairs, dtype=F32) / n_pairs)
    ang = jnp.concatenate([row[:, None] * inv_freq[None, :], col[:, None] * inv_freq[None, :]], axis=-1)
    cos = jnp.repeat(jnp.cos(ang), 2, axis=-1)
    sin = jnp.repeat(jnp.sin(ang), 2, axis=-1)
    even = (jnp.arange(HEAD) % 2 == 0)[None, :]
    return cos, jnp.where(even, -sin, 0.0), jnp.where(even, 0.0, sin)


def _alibi_slopes():
    h = np.arange(1, C_HEADS + 1, dtype=np.float32)
    return [float(v) for v in np.float32(2.0) ** (-np.float32(ALIBI_MAX_EXP) * h / np.float32(C_HEADS))]


def kernel(x_prompt, x_sample, mem_prompt, mem_sample, ffn1_pre_norm, ffn1_w_in, ffn1_w_out, ffn1_post_norm, mix_pre_norm, w_mix_in, a_q_norm, a_k_norm, w_a_proj, w_b_proj, w_c_proj, w_branch_gate, b_branch_gate, w_mix_out, mix_post_norm, xattn_pre_norm, mem_norm, w_xq, w_xkv, w_xo, xattn_post_norm, ffn2_pre_norm, ffn2_w_in, ffn2_w_out, ffn2_post_norm):
    bp, sp, d = x_prompt.shape
    bs, ss, _ = x_sample.shape
    n_mem = mem_prompt.shape[1]
    depth = ffn1_w_in.shape[0]
    tp = bp * sp
    groups = ((0, bp, sp), (tp, bs, ss))
    assert tp % ss == 0 and sp % GRID_W == 0 and ss % GRID_W == 0

    x = (x_prompt.reshape(tp, d), x_sample.reshape(bs * ss, d))
    mem = jnp.concatenate([mem_prompt.reshape(bp * n_mem, d), mem_sample.reshape(bs * n_mem, d)], axis=0)

    rope = _rope_tables(max(sp, ss))
    dft = {s: _dft_tables(s) for s in {sp, ss}}
    kc = np.arange(HEAD)
    ang128 = 2.0 * np.pi * ((kc[:, None] * kc[None, :]) % HEAD) / HEAD
    cs128 = jnp.asarray(np.concatenate([np.cos(ang128), np.sin(ang128)], axis=1), BF16)
    slopes = _alibi_slopes()
    head_scale = jnp.concatenate([jnp.full((A_Q_HEADS, HEAD), SCALE, F32),
                                  jnp.ones((A_KV_HEADS, HEAD), F32)], axis=0)
    qk_gains = jnp.concatenate([jnp.repeat(a_q_norm[:, None, :], A_Q_HEADS, axis=1),
                                jnp.repeat(a_k_norm[:, None, :], A_KV_HEADS, axis=1)], axis=1)

    bf = lambda w: w.astype(BF16)
    ffn1_w_in, ffn1_w_out, ffn2_w_in, ffn2_w_out = bf(ffn1_w_in), bf(ffn1_w_out), bf(ffn2_w_in), bf(ffn2_w_out)
    w_mix_in, w_branch_gate, w_mix_out = bf(w_mix_in), bf(w_branch_gate), bf(w_mix_out)
    w_a_proj, w_b_proj, w_c_proj = bf(w_a_proj), bf(w_b_proj), bf(w_c_proj)
    w_xq, w_xkv, w_xo = bf(w_xq), bf(w_xkv), bf(w_xo)

    nat_block = lambda j: jnp.where(j < NAT_V, BLK_B + j,
                                    jnp.where(j == NAT_V, BLK_V, BLK_CQ + N_PAT * (j - NAT_C)))

    h = rms_rows(x, ffn1_pre_norm, 0)
    for l in range(depth):
        act = swiglu_in(h, ffn1_w_in, l)
        o = matmul(act, ffn1_w_out, l, BF16, tm=512, tn=512, name="ffn_out")
        x, u = resid_norm(o, x, ffn1_post_norm, l, mix_pre_norm, l, 0.5, tp)

        qk = mix_qk(u, w_mix_in, l, qk_gains, head_scale, rope, groups)
        nat = matmul(u, w_mix_in, l, BF16, tn=C_OUT_W, n_out=NAT_W, col_block=nat_block, name="mix_nat")
        y_a = y_b = None
        for g in groups:
            y_a = attn_a(qk, nat, g, y_a)
        zc, zs = chan_dft(nat, cs128)
        for g in groups:
            y_b = seq_dft(zc, zs, *dft[g[2]], g, y_b)
        outs, lses = [], []
        for p_idx, (_, dil) in enumerate(C_PATTERNS):
            if dil == 1:
                qkv, col0 = nat.reshape(1, *nat.shape), NAT_C
            else:
                qkv, col0 = mix_dil(u, w_mix_in, l, p_idx), 0
            o_p, l_p = dil_attn(qkv, col0, p_idx, groups, slopes)
            outs.append(o_p)
            lses.append(l_p)
        y_c = dil_combine(outs, lses)
        merged = merge_branches(u, y_a, y_b, y_c, w_branch_gate, b_branch_gate,
                                w_a_proj, w_b_proj, w_c_proj, l)
        o = matmul(merged, w_mix_out, l, BF16, name="mix_out")
        x, u = resid_norm(o, x, mix_post_norm, l, xattn_pre_norm, l, 1.0, tp)

        q = matmul(u, w_xq, l, BF16, name="xattn_q")
        kv = matmul(rms_rows(mem, mem_norm, l), w_xkv, l, BF16, name="xattn_kv")
        x, h = out_proj_resid(xattn(q, kv, groups, n_mem), w_xo, x, xattn_post_norm, ffn2_pre_norm, l)

        act = swiglu_in(h, ffn2_w_in, l)
        o = matmul(act, ffn2_w_out, l, BF16, tm=512, tn=512, name="ffn_out")
        if l + 1 < depth:
            x, h = resid_norm(o, x, ffn2_post_norm, l, ffn1_pre_norm, l + 1, 0.5, tp)
        else:
            y_p, y_s = resid_norm(o, x, ffn2_post_norm, l, None, None, 0.5, tp)

    return y_p.reshape(bp, sp, d), y_s.reshape(bs, ss, d)
```

```python
import functools
import math

import numpy as np
import jax
import jax.numpy as jnp
from jax import lax
from jax.experimental import pallas as pl
from jax.experimental.pallas import tpu as pltpu

F32 = jnp.float32
BF16 = jnp.bfloat16

HEAD = 128
GRID_W = 64
EPS = 1e-6
ROPE_BASE = 10000.0
A_Q_HEADS = 12
A_KV_HEADS = 4
A_GROUPS = A_Q_HEADS // A_KV_HEADS
B_GROUPS = 20
C_PATTERNS = ((128, 1), (512, 4), (2048, 16))
C_PER = 4
C_HEADS = C_PER * len(C_PATTERNS)
ALIBI_MAX_EXP = 8.0
X_HEADS = 4
N_BRANCH = 3

A_Q_W = A_Q_HEADS * HEAD
A_KV_W = A_KV_HEADS * HEAD
QK_W = A_Q_W + A_KV_W
B_W = B_GROUPS * HEAD
C_W = C_HEADS * HEAD
C_OUT_W = C_PER * HEAD
X_W = X_HEADS * HEAD
MIX_W = A_Q_W + 2 * A_KV_W + B_W + 3 * C_W
BLK_V = QK_W // C_OUT_W
BLK_CQ = (QK_W + A_KV_W + B_W) // C_OUT_W
N_PAT = len(C_PATTERNS)
BLK_B = (QK_W + A_KV_W) // C_OUT_W
NAT_W = B_W + A_KV_W + 3 * C_OUT_W
NAT_V = B_W // C_OUT_W
NAT_C = NAT_V + 1
SCALE = HEAD ** -0.5
NEG = -1e30
HALO = HEAD // 2

VMEM_LIMIT_V7X = 56 * 1024 * 1024
LANE = 128
MXU_N = 256


def _params(n_axes):
    return pltpu.CompilerParams(dimension_semantics=("arbitrary",) * n_axes,
                                vmem_limit_bytes=VMEM_LIMIT_V7X)


def _pick(n, pref, unit=LANE):
    if n <= pref:
        return n
    best = None
    for t in range(unit, pref + 1, unit):
        if n % t == 0:
            best = t
    assert best is not None, (n, pref)
    return best


def _wspec(layer, shape, imap):
    return pl.BlockSpec((None,) + shape, lambda *g: (layer,) + tuple(imap(*g)))


def _rms(x, g):
    return x * lax.rsqrt(jnp.mean(x * x, axis=-1, keepdims=True) + EPS) * g


def _sigmoid(x):
    return 1.0 / (1.0 + jnp.exp(-x))


def _dot(a, b):
    return jnp.dot(a, b, preferred_element_type=F32)


def _dot_nt(a, b):
    return lax.dot_general(a, b, (((1,), (1,)), ((), ())), preferred_element_type=F32)


def _row_inputs(x, tr):
    if not isinstance(x, tuple):
        return [x], [pl.BlockSpec((tr, x.shape[1]), lambda i: (i, 0))], None
    xp, xs = x
    n_p = xp.shape[0] // tr
    d = xp.shape[1]
    return ([xp, xs],
            [pl.BlockSpec((tr, d), lambda i: (jnp.minimum(i, n_p - 1), 0)),
             pl.BlockSpec((tr, d), lambda i: (jnp.maximum(i - n_p, 0), 0))], n_p)


def _load_rows(x_refs, n_p):
    if len(x_refs) == 1:
        return x_refs[0][...]
    return jnp.where(pl.program_id(0) < n_p, x_refs[0][...], x_refs[1][...])


def _rms_kernel(*refs, n_p):
    x_refs, g_ref, o_ref = refs[:-2], refs[-2], refs[-1]
    o_ref[...] = _rms(_load_rows(x_refs, n_p), g_ref[...]).astype(o_ref.dtype)


def rms_rows(x, g, layer, tr=256):
    pieces = x if isinstance(x, tuple) else (x,)
    t = sum(a.shape[0] for a in pieces)
    d = g.shape[-1]
    tr = _pick(math.gcd(*[a.shape[0] for a in pieces]), tr, 8)
    arrs, specs, n_p = _row_inputs(x, tr)
    return pl.pallas_call(
        functools.partial(_rms_kernel, n_p=n_p),
        grid=(t // tr,),
        in_specs=specs + [_wspec(layer, (1, d), lambda i: (0, 0))],
        out_specs=pl.BlockSpec((tr, d), lambda i: (i, 0)),
        out_shape=jax.ShapeDtypeStruct((t, d), BF16),
        compiler_params=_params(1),
        name="rms_rows",
    )(*arrs, g.reshape(g.shape[0], 1, d))


def _resid_kernel(*refs, coef, n_x, n_p, has_next):
    o_ref, x_refs, gp_ref, rest = refs[0], refs[1:1 + n_x], refs[1 + n_x], refs[2 + n_x:]
    xn = _load_rows(x_refs, n_p) + coef * _rms(o_ref[...].astype(F32), gp_ref[...])
    if has_next:
        gn_ref, xo_ref, h_ref = rest
        xo_ref[...] = xn
        h_ref[...] = _rms(xn, gn_ref[...]).astype(h_ref.dtype)
    else:
        yp_ref, ys_ref = rest
        i = pl.program_id(0)

        @pl.when(i < n_p)
        def _():
            yp_ref[...] = xn

        @pl.when(i >= n_p)
        def _():
            ys_ref[...] = xn


def resid_norm(o, x, g_post, layer, g_next, next_layer, coef, rows_p, tr=256):
    t, d = o.shape
    tr = _pick(math.gcd(rows_p, t - rows_p), tr, 8)
    arrs, specs, n_p = _row_inputs(x, tr)
    row = pl.BlockSpec((tr, d), lambda i: (i, 0))
    vec = lambda lyr: _wspec(lyr, (1, d), lambda i: (0, 0))
    g3 = lambda g: g.reshape(g.shape[0], 1, d)
    if g_next is None:
        n_p = rows_p // tr
        return pl.pallas_call(
            functools.partial(_resid_kernel, coef=coef, n_x=len(arrs), n_p=n_p, has_next=False),
            grid=(t // tr,),
            in_specs=[row] + specs + [vec(layer)],
            out_specs=[pl.BlockSpec((tr, d), lambda i: (jnp.minimum(i, n_p - 1), 0)),
                       pl.BlockSpec((tr, d), lambda i: (jnp.maximum(i - n_p, 0), 0))],
            out_shape=[jax.ShapeDtypeStruct((rows_p, d), F32), jax.ShapeDtypeStruct((t - rows_p, d), F32)],
            compiler_params=_params(1),
            name="resid_last",
        )(o, *arrs, g3(g_post))
    return pl.pallas_call(
        functools.partial(_resid_kernel, coef=coef, n_x=len(arrs), n_p=n_p, has_next=True),
        grid=(t // tr,),
        in_specs=[row] + specs + [vec(layer), vec(next_layer)],
        out_specs=[row, row],
        out_shape=[jax.ShapeDtypeStruct((t, d), F32), jax.ShapeDtypeStruct((t, d), BF16)],
        compiler_params=_params(1),
        name="resid_norm",
    )(o, *arrs, g3(g_post), g3(g_next))


def _mm_kernel(x_ref, w_ref, o_ref):
    o_ref[...] = _dot(x_ref[...], w_ref[...]).astype(o_ref.dtype)


def matmul(x, w, layer, out_dtype, tm=1024, tn=1024, n_out=None, col_block=None, name="matmul"):
    m, k = x.shape
    n = w.shape[2] if n_out is None else n_out
    tm = _pick(m, tm, 8)
    tn = _pick(n, tn)
    col_block = col_block or (lambda j: j)
    return pl.pallas_call(
        _mm_kernel,
        grid=(m // tm, n // tn),
        in_specs=[pl.BlockSpec((tm, k), lambda i, j: (i, 0)),
                  _wspec(layer, (k, tn), lambda i, j: (0, col_block(j)))],
        out_specs=pl.BlockSpec((tm, tn), lambda i, j: (i, j)),
        out_shape=jax.ShapeDtypeStruct((m, n), out_dtype),
        compiler_params=_params(2),
        name=name,
    )(x, w)


def _swiglu_kernel(x_ref, wg_ref, wu_ref, o_ref):
    x = x_ref[...]
    g = _dot(x, wg_ref[...].astype(x.dtype))
    u = _dot(x, wu_ref[...].astype(x.dtype))
    o_ref[...] = (g * _sigmoid(g) * u).astype(o_ref.dtype)


def swiglu_in(h, w_in, layer, tm=1024, tn=256):
    t, d = h.shape
    f = w_in.shape[2] // 2
    tm = _pick(t, tm, 8)
    tn = _pick(f, tn)
    nj = f // tn
    return pl.pallas_call(
        _swiglu_kernel,
        grid=(t // tm, nj),
        in_specs=[pl.BlockSpec((tm, d), lambda i, j: (i, 0)),
                  _wspec(layer, (d, tn), lambda i, j: (0, j)),
                  _wspec(layer, (d, tn), lambda i, j: (0, nj + j))],
        out_specs=pl.BlockSpec((tm, tn), lambda i, j: (i, j)),
        out_shape=jax.ShapeDtypeStruct((t, f), BF16),
        compiler_params=_params(2),
        name="swiglu_in",
    )(h, w_in, w_in)


def _merge_kernel(u_ref, a_ref, b_ref, c_ref, wg0_ref, wg1_ref, wg2_ref, bg0_ref, bg1_ref, bg2_ref,
                  wa_ref, wb_ref, wc_ref, o_ref):
    u = u_ref[...]
    acc = _sigmoid(_dot(u, wg0_ref[...]) + bg0_ref[...]) * _dot(a_ref[...], wa_ref[...])
    acc += _sigmoid(_dot(u, wg1_ref[...]) + bg1_ref[...]) * _dot(b_ref[...], wb_ref[...])
    acc += _sigmoid(_dot(u, wg2_ref[...]) + bg2_ref[...]) * _dot(c_ref[...], wc_ref[...])
    o_ref[...] = acc.astype(o_ref.dtype)


def merge_branches(u, a, b, c, w_gate, b_gate, w_a, w_b, w_c, layer, tm=512, tn=256):
    t, d = u.shape
    tm = _pick(t, tm, 8)
    tn = _pick(d, tn)
    nj = d // tn
    act = lambda width: pl.BlockSpec((tm, width), lambda i, j: (i, 0))
    gate_w = lambda br: _wspec(layer, (d, tn), lambda i, j: (0, br * nj + j))
    gate_b = lambda br: _wspec(layer, (1, tn), lambda i, j: (0, br * nj + j))
    proj_w = lambda width: _wspec(layer, (width, tn), lambda i, j: (0, j))
    bg = b_gate.reshape(b_gate.shape[0], 1, N_BRANCH * d)
    return pl.pallas_call(
        _merge_kernel,
        grid=(t // tm, nj),
        in_specs=[act(d), act(A_Q_W), act(B_W), act(C_OUT_W),
                  gate_w(0), gate_w(1), gate_w(2), gate_b(0), gate_b(1), gate_b(2),
                  proj_w(A_Q_W), proj_w(B_W), proj_w(C_OUT_W)],
        out_specs=pl.BlockSpec((tm, tn), lambda i, j: (i, j)),
        out_shape=jax.ShapeDtypeStruct((t, d), BF16),
        compiler_params=_params(2),
        name="merge_branches",
    )(u, a, b, c, w_gate, w_gate, w_gate, bg, bg, bg, w_a, w_b, w_c)


def _out_resid_kernel(a_ref, w_ref, x_ref, gp_ref, gn_ref, xo_ref, h_ref):
    xn = x_ref[...] + _rms(_dot(a_ref[...], w_ref[...]), gp_ref[...])
    xo_ref[...] = xn
    h_ref[...] = _rms(xn, gn_ref[...]).astype(h_ref.dtype)


def out_proj_resid(a, w, x, g_post, g_next, layer, tm=256):
    t, d = x.shape
    k = a.shape[1]
    tm = _pick(t, tm, 8)
    row = pl.BlockSpec((tm, d), lambda i: (i, 0))
    vec = _wspec(layer, (1, d), lambda i: (0, 0))
    g3 = lambda g: g.reshape(g.shape[0], 1, d)
    return pl.pallas_call(
        _out_resid_kernel,
        grid=(t // tm,),
        in_specs=[pl.BlockSpec((tm, k), lambda i: (i, 0)),
                  _wspec(layer, (k, d), lambda i: (0, 0)), row, vec, vec],
        out_specs=[row, row],
        out_shape=[jax.ShapeDtypeStruct((t, d), F32), jax.ShapeDtypeStruct((t, d), BF16)],
        compiler_params=_params(1),
        name="out_proj_resid",
    )(a, w, x, g3(g_post), g3(g_next))


def _mix_qk_kernel(u_ref, w_ref, g_ref, sc_ref, cs_ref, sa_ref, sb_ref, o_ref):
    u = u_ref[...]
    cs, sa, sb = cs_ref[...], sa_ref[...], sb_ref[...]
    for c in range(o_ref.shape[1] // MXU_N):
        y = _dot(u, w_ref[:, c * MXU_N:(c + 1) * MXU_N])
        for hh in range(MXU_N // HEAD):
            h = c * (MXU_N // HEAD) + hh
            x = _rms(y[:, hh * HEAD:(hh + 1) * HEAD], g_ref[h:h + 1, :])
            r = x * cs + pltpu.roll(x, HEAD - 1, 1) * sa + pltpu.roll(x, 1, 1) * sb
            o_ref[:, h * HEAD:(h + 1) * HEAD] = (r * sc_ref[h:h + 1, :]).astype(o_ref.dtype)


def mix_qk(u, w_mix_in, layer, gains, scales, rope, groups, tm=1024, tn=1024):
    t, d = u.shape
    (_, bp, sp), (_, _, ss) = groups
    tm = _pick(math.gcd(sp, ss), tm, 8)
    tn = _pick(QK_W, tn)
    n_p = bp * sp // tm
    hpt = tn // HEAD

    def pos_block(i):
        return jnp.where(i < n_p, i % (sp // tm), (i - n_p) % (ss // tm))

    tab = pl.BlockSpec((tm, HEAD), lambda i, j: (pos_block(i), 0))
    return pl.pallas_call(
        _mix_qk_kernel,
        grid=(t // tm, QK_W // tn),
        in_specs=[pl.BlockSpec((tm, d), lambda i, j: (i, 0)),
                  _wspec(layer, (d, tn), lambda i, j: (0, j)),
                  _wspec(layer, (hpt, HEAD), lambda i, j: (j, 0)),
                  pl.BlockSpec((hpt, HEAD), lambda i, j: (j, 0)),
                  tab, tab, tab],
        out_specs=pl.BlockSpec((tm, tn), lambda i, j: (i, j)),
        out_shape=jax.ShapeDtypeStruct((t, QK_W), BF16),
        compiler_params=_params(2),
        name="mix_qk",
    )(u, w_mix_in, gains, scales, *rope)


def _mix_dil_kernel(u_ref, w_ref, o_ref, scr_ref, *, dil):
    u = u_ref[...]
    nchunk, tm, _ = scr_ref.shape
    per = MXU_N // LANE
    for cc in range(nchunk // per):
        y = _dot(u, w_ref[:, cc * MXU_N:(cc + 1) * MXU_N])
        for c in range(cc * per, (cc + 1) * per):
            scr_ref[c] = y[:, (c - cc * per) * LANE:(c - cc * per + 1) * LANE]
            for r in range(dil):
                o_ref[r, :, c * LANE:(c + 1) * LANE] = (
                    scr_ref[c, pl.ds(r, tm // dil, stride=dil), :].astype(o_ref.dtype))


def mix_dil(u, w_mix_in, layer, p_idx, tm=1024):
    t, d = u.shape
    dil = C_PATTERNS[p_idx][1]
    tm = _pick(t, tm, 8 * dil)
    return pl.pallas_call(
        functools.partial(_mix_dil_kernel, dil=dil),
        grid=(t // tm, 3),
        in_specs=[pl.BlockSpec((tm, d), lambda i, j: (i, 0)),
                  _wspec(layer, (d, C_OUT_W), lambda i, j: (0, BLK_CQ + p_idx + N_PAT * j))],
        out_specs=pl.BlockSpec((dil, tm // dil, C_OUT_W), lambda i, j: (0, i, j)),
        out_shape=jax.ShapeDtypeStruct((dil, t // dil, 3 * C_OUT_W), BF16),
        scratch_shapes=[pltpu.VMEM((C_OUT_W // LANE, tm, LANE), F32)],
        compiler_params=_params(2),
        name=f"mix_dil_p{p_idx}",
    )(u, w_mix_in)


def _attn_a_kernel(q_ref, k_ref, v_ref, *rest):
    o_ref = rest[-1]
    k = k_ref[...]
    v = v_ref[...]
    for g in range(A_GROUPS):
        sl = slice(g * HEAD, (g + 1) * HEAD)
        s = _dot_nt(q_ref[:, sl], k)
        p = jnp.exp(s - jnp.max(s, axis=-1, keepdims=True))
        l = jnp.sum(p, axis=-1, keepdims=True)
        o = _dot(p.astype(BF16), v) * (1.0 / l)
        o_ref[:, sl] = o.astype(o_ref.dtype)


def attn_a(qk, nat, group, prev_out, tq=256):
    row0, nseq, s = group
    t = qk.shape[0]
    tq = _pick(s, tq, 8)
    nq = s // tq
    rb0, sb0 = row0 // tq, row0 // s
    gw = A_GROUPS * HEAD
    in_specs = [pl.BlockSpec((tq, gw), lambda b, h, i: (rb0 + b * nq + i, h)),
                pl.BlockSpec((s, HEAD), lambda b, h, i: (sb0 + b, A_Q_HEADS + h)),
                pl.BlockSpec((s, HEAD), lambda b, h, i: (sb0 + b, NAT_V * (C_OUT_W // HEAD) + h))]
    args = [qk, qk, nat]
    aliases = {}
    if prev_out is not None:
        in_specs.append(pl.BlockSpec(memory_space=pl.ANY))
        args.append(prev_out)
        aliases = {3: 0}
    return pl.pallas_call(
        _attn_a_kernel,
        grid=(nseq, A_KV_HEADS, nq),
        in_specs=in_specs,
        out_specs=pl.BlockSpec((tq, gw), lambda b, h, i: (rb0 + b * nq + i, h)),
        out_shape=jax.ShapeDtypeStruct((t, A_Q_W), BF16),
        input_output_aliases=aliases,
        compiler_params=_params(3),
        name="attn_a",
    )(*args)


RADIX = 8
DFT_TM = 256


def _chan_dft_kernel(z_ref, cs_ref, zc_ref, zs_ref, scr_ref):
    cs = cs_ref[...]
    rows = zc_ref.shape[1]
    for g in range(B_GROUPS):
        sl = slice(g * HEAD, (g + 1) * HEAD)
        y = _dot(z_ref[:, sl], cs) * (HEAD ** -0.5)
        scr_ref[2 * g] = y[:, :HEAD]
        scr_ref[2 * g + 1] = y[:, HEAD:]
        for b in range(RADIX):
            zc_ref[b, :, sl] = scr_ref[2 * g, pl.ds(b, rows, stride=RADIX), :].astype(zc_ref.dtype)
            zs_ref[b, :, sl] = scr_ref[2 * g + 1, pl.ds(b, rows, stride=RADIX), :].astype(zs_ref.dtype)


def chan_dft(nat, cs128, tr=512):
    t = nat.shape[0]
    tr = _pick(t, tr, 16 * RADIX)
    out = pl.BlockSpec((RADIX, tr // RADIX, B_W), lambda i: (0, i, 0))
    return pl.pallas_call(
        _chan_dft_kernel,
        grid=(t // tr,),
        in_specs=[pl.BlockSpec((tr, B_W), lambda i: (i, 0)),
                  pl.BlockSpec((HEAD, 2 * HEAD), lambda i: (0, 0))],
        out_specs=[out, out],
        out_shape=[jax.ShapeDtypeStruct((RADIX, t // RADIX, B_W), BF16)] * 2,
        scratch_shapes=[pltpu.VMEM((2 * B_GROUPS, tr, LANE), F32)],
        compiler_params=_params(1),
        name="chan_dft",
    )(nat, cs128)


def _outer_dft_real(tre, tim, k2):
    sums = {}
    for b in range(RADIX):
        ang = 2.0 * math.pi * ((k2 * b) % RADIX) / RADIX
        for coef, tile in ((math.cos(ang), tre[b]), (math.sin(ang), tim[b])):
            mag = round(abs(coef), 9)
            if mag == 0.0:
                continue
            term = tile if coef > 0 else -tile
            sums[mag] = term if mag not in sums else sums[mag] + term
    return sum(v if mag == 1.0 else v * mag for mag, v in sums.items())


def _seq_dft_kernel(m_ref, zc_ref, zs_ref, *rest, scale):
    o_ref = rest[-1]
    a = zc_ref.shape[1]
    tm = m_ref.shape[1] // 2
    tre, tim = [], []
    for b in range(RADIX):
        t = _dot(m_ref[b, :, :a], zc_ref[b]) + _dot(m_ref[b, :, a:], zs_ref[b])
        tre.append(t[:tm])
        tim.append(t[tm:])
    for k2 in range(RADIX):
        o_ref[k2] = (_outer_dft_real(tre, tim, k2) * scale).astype(o_ref.dtype)


def seq_dft(zc, zs, tables, group, prev_out, tn=256):
    row0, nseq, s = group
    t = zc.shape[1] * RADIX
    a = s // RADIX
    ni, tm2 = tables.shape[1], tables.shape[2]
    tm = tm2 // 2
    tn = _pick(B_W, tn)
    nj = B_W // tn
    sb0 = row0 // s
    zspec = pl.BlockSpec((RADIX, a, tn), lambda i, b, j: (0, sb0 + b, j))
    in_specs = [pl.BlockSpec((RADIX, None, tm2, 2 * a), lambda i, b, j: (0, i, 0, 0)), zspec, zspec]
    args = [tables, zc, zs]
    aliases = {}
    out_shape = (t // a, ni, tm, B_W)
    if prev_out is not None:
        in_specs.append(pl.BlockSpec(memory_space=pl.ANY))
        args.append(prev_out.reshape(out_shape))
        aliases = {3: 0}
    return pl.pallas_call(
        functools.partial(_seq_dft_kernel, scale=s ** -0.5),
        grid=(ni, nseq, nj),
        in_specs=in_specs,
        out_specs=pl.BlockSpec((RADIX, None, tm, tn), lambda i, b, j: (sb0 + b, i, 0, j)),
        out_shape=jax.ShapeDtypeStruct(out_shape, BF16),
        input_output_aliases=aliases,
        compiler_params=_params(3),
        name="seq_dft",
    )(*args).reshape(t, B_W)


def _dft_tables(s):
    a = s // RADIX
    tm = min(a, DFT_TM)
    k1 = jnp.arange(a, dtype=jnp.int32)
    n = RADIX * jnp.arange(a, dtype=jnp.int32)[None, :] + jnp.arange(RADIX, dtype=jnp.int32)[:, None]
    ang = ((k1[None, :, None] * n[:, None, :]) % s).astype(F32) * (2.0 * math.pi / s)
    c, sn = jnp.cos(ang), jnp.sin(ang)
    tile = lambda m: m.reshape(RADIX, a // tm, tm, 2 * a)
    re = tile(jnp.concatenate([c, -sn], axis=-1))
    im = tile(jnp.concatenate([-sn, -c], axis=-1))
    return jnp.concatenate([re, im], axis=2).astype(BF16)


def _dil_kernel(q_ref, kp_ref, km_ref, kn_ref, vp_ref, vm_ref, vn_ref, o_ref, l_ref, *,
                dil, half, rows, n_p, sub_p, sub_s, slopes):
    i = pl.program_id(1)
    start = i * rows
    in_p = start < n_p
    seq_len = jnp.where(in_p, sub_p, sub_s)
    local = jnp.where(in_p, start % sub_p, (start - n_p) % sub_s)
    qi = lax.broadcasted_iota(jnp.int32, (HEAD, 2 * HEAD), 0)
    ci = lax.broadcasted_iota(jnp.int32, (HEAD, 2 * HEAD), 1)
    rel = ci - HALO - qi
    band = jnp.abs(rel) <= half
    dist = (jnp.abs(rel) * dil).astype(F32)
    nsub = rows // HEAD

    def window(main_ref, prev_ref, next_ref, r, s, sl):
        lo, hi = s * HEAD - HALO, (s + 1) * HEAD + HALO
        parts = [prev_ref[r, :, sl]] if lo < 0 else []
        parts.append(main_ref[r, max(lo, 0):min(hi, rows), sl])
        if hi > rows:
            parts.append(next_ref[r, :, sl])
        return parts[0] if len(parts) == 1 else jnp.concatenate(parts, axis=0)

    for s in range(nsub):
        kpos = local + s * HEAD - HALO + ci
        valid = band & (kpos >= 0) & (kpos < seq_len)
        qs = slice(s * HEAD, (s + 1) * HEAD)
        for h in range(C_PER):
            sl = slice(h * HEAD, (h + 1) * HEAD)
            bias = jnp.where(valid, -slopes[h] * dist, NEG)
            for r in range(q_ref.shape[0]):
                sc = _dot_nt(q_ref[r, qs, sl], window(km_ref, kp_ref, kn_ref, r, s, sl))
                sc = sc * SCALE + bias
                m = jnp.max(sc, axis=-1, keepdims=True)
                e = jnp.exp(sc - m)
                den = jnp.sum(e, axis=-1, keepdims=True)
                p = (e * (1.0 / den)).astype(BF16)
                o_ref[r, qs, sl] = _dot(p, window(vm_ref, vp_ref, vn_ref, r, s, sl))
                l_ref[r, qs, sl] = jnp.broadcast_to(m + jnp.log(den), (HEAD, HEAD))


def dil_attn(qkv, col0, p_idx, groups, slopes, rows_per_step=512):
    window, dil = C_PATTERNS[p_idx]
    half = window // (2 * dil)
    assert half <= HALO
    (_, bp, sp), (_, bs, ss) = groups
    n = qkv.shape[1]
    n_p, sub_p, sub_s = bp * sp // dil, sp // dil, ss // dil
    assert sub_p % HEAD == 0 and sub_s % HEAD == 0
    rows = _pick(math.gcd(sub_p, sub_s), rows_per_step)
    per = rows // HALO
    nhalo = n // HALO
    nres = math.gcd(dil, max(1, rows_per_step // rows))

    main = lambda c: pl.BlockSpec((nres, rows, C_OUT_W), lambda r, i: (r, i, col0 + c))
    prev = lambda c: pl.BlockSpec((nres, HALO, C_OUT_W),
                                  lambda r, i: (r, jnp.maximum(i * per - 1, 0), col0 + c))
    nxt = lambda c: pl.BlockSpec((nres, HALO, C_OUT_W),
                                 lambda r, i: (r, jnp.minimum((i + 1) * per, nhalo - 1), col0 + c))
    out_spec = pl.BlockSpec((nres, rows, C_OUT_W), lambda r, i: (r, i, 0))
    out_sds = jax.ShapeDtypeStruct((dil, n, C_OUT_W), F32)
    return pl.pallas_call(
        functools.partial(_dil_kernel, dil=dil, half=half, rows=rows, n_p=n_p, sub_p=sub_p, sub_s=sub_s,
                          slopes=tuple(slopes[p_idx * C_PER:(p_idx + 1) * C_PER])),
        grid=(dil // nres, n // rows),
        in_specs=[main(0), prev(1), main(1), nxt(1), prev(2), main(2), nxt(2)],
        out_specs=[out_spec, out_spec],
        out_shape=[out_sds, out_sds],
        compiler_params=_params(2),
        name=f"dil_attn_p{p_idx}",
    )(*([qkv] * 7))


def _dil_combine_kernel(*refs):
    srcs, out_ref, scr = refs[:2 * N_PAT], refs[2 * N_PAT], list(refs[2 * N_PAT + 1:])
    vals = []
    for src in srcs:
        dil = src.shape[0]
        if dil == 1:
            vals.append(src[0])
            continue
        nat = scr.pop(0)
        nchunk = nat.shape[0]
        for r in range(dil):
            for c in range(nchunk):
                nat[c, pl.ds(r, src.shape[1], stride=dil), :] = src[r, :, c * LANE:(c + 1) * LANE]
        vals.append(jnp.concatenate([nat[c] for c in range(nchunk)], axis=-1))
    outs, lses = vals[:N_PAT], vals[N_PAT:]
    m = functools.reduce(jnp.maximum, lses)
    es = [jnp.exp(l - m) for l in lses]
    num = sum(e * o for e, o in zip(es, outs))
    out_ref[...] = (num / sum(es)).astype(out_ref.dtype)


def dil_combine(outs, lses, tr=512):
    t = outs[0].shape[0] * outs[0].shape[1]
    tr = _pick(t, tr, 8 * max(d for _, d in C_PATTERNS))
    spec = lambda a: pl.BlockSpec((a.shape[0], tr // a.shape[0], C_OUT_W), lambda i: (0, i, 0))
    n_scr = sum(2 for a in outs if a.shape[0] > 1)
    return pl.pallas_call(
        _dil_combine_kernel,
        grid=(t // tr,),
        in_specs=[spec(a) for a in outs + lses],
        out_specs=pl.BlockSpec((tr, C_OUT_W), lambda i: (i, 0)),
        out_shape=jax.ShapeDtypeStruct((t, C_OUT_W), BF16),
        scratch_shapes=[pltpu.VMEM((C_OUT_W // LANE, tr, LANE), F32)] * n_scr,
        compiler_params=_params(1),
        name="dil_combine",
    )(*outs, *lses)


def _xattn_kernel(q_ref, kv_ref, o_ref):
    for h in range(X_HEADS):
        sl = slice(h * HEAD, (h + 1) * HEAD)
        s = _dot_nt(q_ref[:, sl], kv_ref[:, sl]) * SCALE
        p = jnp.exp(s - jnp.max(s, axis=-1, keepdims=True))
        l = jnp.sum(p, axis=-1, keepdims=True)
        o = _dot(p.astype(BF16), kv_ref[:, X_W + h * HEAD:X_W + (h + 1) * HEAD]) / l
        o_ref[:, sl] = o.astype(o_ref.dtype)


def xattn(q, kv, groups, n_mem, tq=512):
    t = q.shape[0]
    (_, bp, sp), (_, _, ss) = groups
    tq = _pick(math.gcd(sp, ss), tq, 8)
    n_p = bp * sp // tq

    def seq_of(i):
        return jnp.where(i < n_p, i // (sp // tq), bp + (i - n_p) // (ss // tq))

    return pl.pallas_call(
        _xattn_kernel,
        grid=(t // tq,),
        in_specs=[pl.BlockSpec((tq, X_W), lambda i: (i, 0)),
                  pl.BlockSpec((n_mem, 2 * X_W), lambda i: (seq_of(i), 0))],
        out_specs=pl.BlockSpec((tq, X_W), lambda i: (i, 0)),
        out_shape=jax.ShapeDtypeStruct((t, X_W), BF16),
        compiler_params=_params(1),
        name="xattn",
    )(q, kv)


def _rope_tables(seq_len):
    t = jnp.arange(seq_len)
    row = (t // GRID_W).astype(F32)
    col = (t % GRID_W).astype(F32)
    n_pairs = HEAD // 4
    inv_freq = ROPE_BASE ** (-jnp.arange(n_pairs, dtype=F32) / n_pairs)
    ang = jnp.concatenate([row[:, None] * inv_freq[None, :], col[:, None] * inv_freq[None, :]], axis=-1)
    cos = jnp.repeat(jnp.cos(ang), 2, axis=-1)
    sin = jnp.repeat(jnp.sin(ang), 2, axis=-1)
    even = (jnp.arange(HEAD) % 2 == 0)[None, :]
    return cos, jnp.where(even, -sin, 0.0), jnp.where(even, 0.0, sin)


def _alibi_slopes():
    h = np.arange(1, C_HEADS + 1, dtype=np.float32)
    return [float(v) for v in np.float32(2.0) ** (-np.float32(ALIBI_MAX_EXP) * h / np.float32(C_HEADS))]


def kernel(x_prompt, x_sample, mem_prompt, mem_sample, ffn1_pre_norm, ffn1_w_in, ffn1_w_out, ffn1_post_norm, mix_pre_norm, w_mix_in, a_q_norm, a_k_norm, w_a_proj, w_b_proj, w_c_proj, w_branch_gate, b_branch_gate, w_mix_out, mix_post_norm, xattn_pre_norm, mem_norm, w_xq, w_xkv, w_xo, xattn_post_norm, ffn2_pre_norm, ffn2_w_in, ffn2_w_out, ffn2_post_norm):
    bp, sp, d = x_prompt.shape
    bs, ss, _ = x_sample.shape
    n_mem = mem_prompt.shape[1]
    depth = ffn1_w_in.shape[0]
    tp = bp * sp
    groups = ((0, bp, sp), (tp, bs, ss))
    assert tp % ss == 0 and sp % GRID_W == 0 and ss % GRID_W == 0

    x = (x_prompt.reshape(tp, d), x_sample.reshape(bs * ss, d))
    mem = jnp.concatenate([mem_prompt.reshape(bp * n_mem, d), mem_sample.reshape(bs * n_mem, d)], axis=0)

    rope = _rope_tables(max(sp, ss))
    dft = {s: _dft_tables(s) for s in {sp, ss}}
    kc = np.arange(HEAD)
    ang128 = 2.0 * np.pi * ((kc[:, None] * kc[None, :]) % HEAD) / HEAD
    cs128 = jnp.asarray(np.concatenate([np.cos(ang128), np.sin(ang128)], axis=1), BF16)
    slopes = _alibi_slopes()
    head_scale = jnp.concatenate([jnp.full((A_Q_HEADS, HEAD), SCALE, F32),
                                  jnp.ones((A_KV_HEADS, HEAD), F32)], axis=0)
    qk_gains = jnp.concatenate([jnp.repeat(a_q_norm[:, None, :], A_Q_HEADS, axis=1),
                                jnp.repeat(a_k_norm[:, None, :], A_KV_HEADS, axis=1)], axis=1)

    bf = lambda w: w.astype(BF16)
    ffn1_w_out, ffn2_w_out = bf(ffn1_w_out), bf(ffn2_w_out)
    w_mix_in, w_branch_gate, w_mix_out = bf(w_mix_in), bf(w_branch_gate), bf(w_mix_out)
    w_a_proj, w_b_proj, w_c_proj = bf(w_a_proj), bf(w_b_proj), bf(w_c_proj)
    w_xq, w_xkv, w_xo = bf(w_xq), bf(w_xkv), bf(w_xo)

    nat_block = lambda j: jnp.where(j < NAT_V, BLK_B + j,
                                    jnp.where(j == NAT_V, BLK_V, BLK_CQ + N_PAT * (j - NAT_C)))

    h = rms_rows(x, ffn1_pre_norm, 0)
    for l in range(depth):
        act = swiglu_in(h, ffn1_w_in, l)
        o = matmul(act, ffn1_w_out, l, BF16, tm=512, tn=512, name="ffn_out")
        x, u = resid_norm(o, x, ffn1_post_norm, l, mix_pre_norm, l, 0.5, tp)

        qk = mix_qk(u, w_mix_in, l, qk_gains, head_scale, rope, groups)
        nat = matmul(u, w_mix_in, l, BF16, tn=C_OUT_W, n_out=NAT_W, col_block=nat_block, name="mix_nat")
        y_a = y_b = None
        for g in groups:
            y_a = attn_a(qk, nat, g, y_a)
        zc, zs = chan_dft(nat, cs128)
        for g in groups:
            y_b = seq_dft(zc, zs, dft[g[2]], g, y_b)
        outs, lses = [], []
        for p_idx, (_, dil) in enumerate(C_PATTERNS):
            if dil == 1:
                qkv, col0 = nat.reshape(1, *nat.shape), NAT_C
            else:
                qkv, col0 = mix_dil(u, w_mix_in, l, p_idx), 0
            o_p, l_p = dil_attn(qkv, col0, p_idx, groups, slopes)
            outs.append(o_p)
            lses.append(l_p)
        y_c = dil_combine(outs, lses)
        merged = merge_branches(u, y_a, y_b, y_c, w_branch_gate, b_branch_gate,
                                w_a_proj, w_b_proj, w_c_proj, l)
        o = matmul(merged, w_mix_out, l, BF16, name="mix_out")
        x, u = resid_norm(o, x, mix_post_norm, l, xattn_pre_norm, l, 1.0, tp)

        q = matmul(u, w_xq, l, BF16, name="xattn_q")
        kv = matmul(rms_rows(mem, mem_norm, l), w_xkv, l, BF16, name="xattn_kv")
        x, h = out_proj_resid(xattn(q, kv, groups, n_mem), w_xo, x, xattn_post_norm, ffn2_pre_norm, l)

        act = swiglu_in(h, ffn2_w_in, l)
        o = matmul(act, ffn2_w_out, l, BF16, tm=512, tn=512, name="ffn_out")
        if l + 1 < depth:
            x, h = resid_norm(o, x, ffn2_post_norm, l, ffn1_pre_norm, l + 1, 0.5, tp)
        else:
            y_p, y_s = resid_norm(o, x, ffn2_post_norm, l, None, None, 0.5, tp)

    return y_p.reshape(bp, sp, d), y_s.reshape(bs, ss, d)
```

```python
import functools
import math

import numpy as np
import jax
import jax.numpy as jnp
from jax import lax
from jax.experimental import pallas as pl
from jax.experimental.pallas import tpu as pltpu

F32 = jnp.float32
BF16 = jnp.bfloat16

HEAD = 128
GRID_W = 64
EPS = 1e-6
ROPE_BASE = 10000.0
A_Q_HEADS = 12
A_KV_HEADS = 4
A_GROUPS = A_Q_HEADS // A_KV_HEADS
B_GROUPS = 20
C_PATTERNS = ((128, 1), (512, 4), (2048, 16))
C_PER = 4
C_HEADS = C_PER * len(C_PATTERNS)
ALIBI_MAX_EXP = 8.0
X_HEADS = 4
N_BRANCH = 3

A_Q_W = A_Q_HEADS * HEAD
A_KV_W = A_KV_HEADS * HEAD
QK_W = A_Q_W + A_KV_W
B_W = B_GROUPS * HEAD
C_W = C_HEADS * HEAD
C_OUT_W = C_PER * HEAD
X_W = X_HEADS * HEAD
MIX_W = A_Q_W + 2 * A_KV_W + B_W + 3 * C_W
BLK_V = QK_W // C_OUT_W
BLK_CQ = (QK_W + A_KV_W + B_W) // C_OUT_W
N_PAT = len(C_PATTERNS)
BLK_B = (QK_W + A_KV_W) // C_OUT_W
NAT_W = B_W + A_KV_W + 3 * C_OUT_W
NAT_V = B_W // C_OUT_W
NAT_C = NAT_V + 1
SCALE = HEAD ** -0.5
NEG = -1e30
HALO = HEAD // 2

VMEM_LIMIT_V7X = 56 * 1024 * 1024
LANE = 128


def _params(n_axes):
    return pltpu.CompilerParams(dimension_semantics=("arbitrary",) * n_axes,
                                vmem_limit_bytes=VMEM_LIMIT_V7X)


def _pick(n, pref, unit=LANE):
    if n <= pref:
        return n
    best = None
    for t in range(unit, pref + 1, unit):
        if n % t == 0:
            best = t
    assert best is not None, (n, pref)
    return best


def _wspec(layer, shape, imap):
    return pl.BlockSpec((None,) + shape, lambda *g: (layer,) + tuple(imap(*g)))


def _rms(x, g):
    return x * lax.rsqrt(jnp.mean(x * x, axis=-1, keepdims=True) + EPS) * g


def _sigmoid(x):
    return 1.0 / (1.0 + jnp.exp(-x))


def _dot(a, b):
    return jnp.dot(a, b, preferred_element_type=F32)


def _dot_nt(a, b):
    return lax.dot_general(a, b, (((1,), (1,)), ((), ())), preferred_element_type=F32)


def _row_inputs(x, tr):
    if not isinstance(x, tuple):
        return [x], [pl.BlockSpec((tr, x.shape[1]), lambda i: (i, 0))], None
    xp, xs = x
    n_p = xp.shape[0] // tr
    d = xp.shape[1]
    return ([xp, xs],
            [pl.BlockSpec((tr, d), lambda i: (jnp.minimum(i, n_p - 1), 0)),
             pl.BlockSpec((tr, d), lambda i: (jnp.maximum(i - n_p, 0), 0))], n_p)


def _load_rows(x_refs, n_p):
    if len(x_refs) == 1:
        return x_refs[0][...]
    return jnp.where(pl.program_id(0) < n_p, x_refs[0][...], x_refs[1][...])


def _rms_kernel(*refs, n_p):
    x_refs, g_ref, o_ref = refs[:-2], refs[-2], refs[-1]
    o_ref[...] = _rms(_load_rows(x_refs, n_p), g_ref[...]).astype(o_ref.dtype)


def rms_rows(x, g, layer, tr=256):
    pieces = x if isinstance(x, tuple) else (x,)
    t = sum(a.shape[0] for a in pieces)
    d = g.shape[-1]
    tr = _pick(math.gcd(*[a.shape[0] for a in pieces]), tr, 8)
    arrs, specs, n_p = _row_inputs(x, tr)
    return pl.pallas_call(
        functools.partial(_rms_kernel, n_p=n_p),
        grid=(t // tr,),
        in_specs=specs + [_wspec(layer, (1, d), lambda i: (0, 0))],
        out_specs=pl.BlockSpec((tr, d), lambda i: (i, 0)),
        out_shape=jax.ShapeDtypeStruct((t, d), BF16),
        compiler_params=_params(1),
        name="rms_rows",
    )(*arrs, g.reshape(g.shape[0], 1, d))


def _resid_kernel(*refs, coef, n_x, n_p, has_next):
    o_ref, x_refs, gp_ref, rest = refs[0], refs[1:1 + n_x], refs[1 + n_x], refs[2 + n_x:]
    xn = _load_rows(x_refs, n_p) + coef * _rms(o_ref[...].astype(F32), gp_ref[...])
    if has_next:
        gn_ref, xo_ref, h_ref = rest
        xo_ref[...] = xn
        h_ref[...] = _rms(xn, gn_ref[...]).astype(h_ref.dtype)
    else:
        yp_ref, ys_ref = rest
        i = pl.program_id(0)

        @pl.when(i < n_p)
        def _():
            yp_ref[...] = xn

        @pl.when(i >= n_p)
        def _():
            ys_ref[...] = xn


def resid_norm(o, x, g_post, layer, g_next, next_layer, coef, rows_p, tr=256):
    t, d = o.shape
    tr = _pick(math.gcd(rows_p, t - rows_p), tr, 8)
    arrs, specs, n_p = _row_inputs(x, tr)
    row = pl.BlockSpec((tr, d), lambda i: (i, 0))
    vec = lambda lyr: _wspec(lyr, (1, d), lambda i: (0, 0))
    g3 = lambda g: g.reshape(g.shape[0], 1, d)
    if g_next is None:
        n_p = rows_p // tr
        return pl.pallas_call(
            functools.partial(_resid_kernel, coef=coef, n_x=len(arrs), n_p=n_p, has_next=False),
            grid=(t // tr,),
            in_specs=[row] + specs + [vec(layer)],
            out_specs=[pl.BlockSpec((tr, d), lambda i: (jnp.minimum(i, n_p - 1), 0)),
                       pl.BlockSpec((tr, d), lambda i: (jnp.maximum(i - n_p, 0), 0))],
            out_shape=[jax.ShapeDtypeStruct((rows_p, d), F32), jax.ShapeDtypeStruct((t - rows_p, d), F32)],
            compiler_params=_params(1),
            name="resid_last",
        )(o, *arrs, g3(g_post))
    return pl.pallas_call(
        functools.partial(_resid_kernel, coef=coef, n_x=len(arrs), n_p=n_p, has_next=True),
        grid=(t // tr,),
        in_specs=[row] + specs + [vec(layer), vec(next_layer)],
        out_specs=[row, row],
        out_shape=[jax.ShapeDtypeStruct((t, d), F32), jax.ShapeDtypeStruct((t, d), BF16)],
        compiler_params=_params(1),
        name="resid_norm",
    )(o, *arrs, g3(g_post), g3(g_next))


def _mm_kernel(x_ref, w_ref, o_ref):
    o_ref[...] = _dot(x_ref[...], w_ref[...]).astype(o_ref.dtype)


def matmul(x, w, layer, out_dtype, tm=1024, tn=1024, n_out=None, col_block=None, name="matmul"):
    m, k = x.shape
    n = w.shape[2] if n_out is None else n_out
    tm = _pick(m, tm, 8)
    tn = _pick(n, tn)
    col_block = col_block or (lambda j: j)
    return pl.pallas_call(
        _mm_kernel,
        grid=(m // tm, n // tn),
        in_specs=[pl.BlockSpec((tm, k), lambda i, j: (i, 0)),
                  _wspec(layer, (k, tn), lambda i, j: (0, col_block(j)))],
        out_specs=pl.BlockSpec((tm, tn), lambda i, j: (i, j)),
        out_shape=jax.ShapeDtypeStruct((m, n), out_dtype),
        compiler_params=_params(2),
        name=name,
    )(x, w)


def _swiglu_kernel(x_ref, wg_ref, wu_ref, o_ref):
    x = x_ref[...]
    g = _dot(x, wg_ref[...].astype(x.dtype))
    u = _dot(x, wu_ref[...].astype(x.dtype))
    o_ref[...] = (g * _sigmoid(g) * u).astype(o_ref.dtype)


def swiglu_in(h, w_in, layer, tm=2048, tn=256):
    t, d = h.shape
    f = w_in.shape[2] // 2
    tm = _pick(t, tm, 8)
    tn = _pick(f, tn)
    nj = f // tn
    return pl.pallas_call(
        _swiglu_kernel,
        grid=(t // tm, nj),
        in_specs=[pl.BlockSpec((tm, d), lambda i, j: (i, 0)),
                  _wspec(layer, (d, tn), lambda i, j: (0, j)),
                  _wspec(layer, (d, tn), lambda i, j: (0, nj + j))],
        out_specs=pl.BlockSpec((tm, tn), lambda i, j: (i, j)),
        out_shape=jax.ShapeDtypeStruct((t, f), BF16),
        compiler_params=_params(2),
        name="swiglu_in",
    )(h, w_in, w_in)


def _merge_kernel(u_ref, a_ref, b_ref, c_ref, wg0_ref, wg1_ref, wg2_ref, bg0_ref, bg1_ref, bg2_ref,
                  wa_ref, wb_ref, wc_ref, o_ref):
    u = u_ref[...]
    acc = _sigmoid(_dot(u, wg0_ref[...]) + bg0_ref[...]) * _dot(a_ref[...], wa_ref[...])
    acc += _sigmoid(_dot(u, wg1_ref[...]) + bg1_ref[...]) * _dot(b_ref[...], wb_ref[...])
    acc += _sigmoid(_dot(u, wg2_ref[...]) + bg2_ref[...]) * _dot(c_ref[...], wc_ref[...])
    o_ref[...] = acc.astype(o_ref.dtype)


def merge_branches(u, a, b, c, w_gate, b_gate, w_a, w_b, w_c, layer, tm=512, tn=256):
    t, d = u.shape
    tm = _pick(t, tm, 8)
    tn = _pick(d, tn)
    nj = d // tn
    act = lambda width: pl.BlockSpec((tm, width), lambda i, j: (i, 0))
    gate_w = lambda br: _wspec(layer, (d, tn), lambda i, j: (0, br * nj + j))
    gate_b = lambda br: _wspec(layer, (1, tn), lambda i, j: (0, br * nj + j))
    proj_w = lambda width: _wspec(layer, (width, tn), lambda i, j: (0, j))
    bg = b_gate.reshape(b_gate.shape[0], 1, N_BRANCH * d)
    return pl.pallas_call(
        _merge_kernel,
        grid=(t // tm, nj),
        in_specs=[act(d), act(A_Q_W), act(B_W), act(C_OUT_W),
                  gate_w(0), gate_w(1), gate_w(2), gate_b(0), gate_b(1), gate_b(2),
                  proj_w(A_Q_W), proj_w(B_W), proj_w(C_OUT_W)],
        out_specs=pl.BlockSpec((tm, tn), lambda i, j: (i, j)),
        out_shape=jax.ShapeDtypeStruct((t, d), BF16),
        compiler_params=_params(2),
        name="merge_branches",
    )(u, a, b, c, w_gate, w_gate, w_gate, bg, bg, bg, w_a, w_b, w_c)


def _mix_qk_kernel(u_ref, w_ref, g_ref, sc_ref, cs_ref, sa_ref, sb_ref, o_ref):
    y = _dot(u_ref[...], w_ref[...])
    cs, sa, sb = cs_ref[...], sa_ref[...], sb_ref[...]
    for h in range(o_ref.shape[1] // HEAD):
        sl = slice(h * HEAD, (h + 1) * HEAD)
        x = _rms(y[:, sl], g_ref[h:h + 1, :])
        r = x * cs + pltpu.roll(x, HEAD - 1, 1) * sa + pltpu.roll(x, 1, 1) * sb
        o_ref[:, sl] = (r * sc_ref[h:h + 1, :]).astype(o_ref.dtype)


def mix_qk(u, w_mix_in, layer, gains, scales, rope, groups, tm=1024, tn=1024):
    t, d = u.shape
    (_, bp, sp), (_, _, ss) = groups
    tm = _pick(math.gcd(sp, ss), tm, 8)
    tn = _pick(QK_W, tn)
    n_p = bp * sp // tm
    hpt = tn // HEAD

    def pos_block(i):
        return jnp.where(i < n_p, i % (sp // tm), (i - n_p) % (ss // tm))

    tab = pl.BlockSpec((tm, HEAD), lambda i, j: (pos_block(i), 0))
    return pl.pallas_call(
        _mix_qk_kernel,
        grid=(t // tm, QK_W // tn),
        in_specs=[pl.BlockSpec((tm, d), lambda i, j: (i, 0)),
                  _wspec(layer, (d, tn), lambda i, j: (0, j)),
                  _wspec(layer, (hpt, HEAD), lambda i, j: (j, 0)),
                  pl.BlockSpec((hpt, HEAD), lambda i, j: (j, 0)),
                  tab, tab, tab],
        out_specs=pl.BlockSpec((tm, tn), lambda i, j: (i, j)),
        out_shape=jax.ShapeDtypeStruct((t, QK_W), BF16),
        compiler_params=_params(2),
        name="mix_qk",
    )(u, w_mix_in, gains, scales, *rope)


def _mix_dil_kernel(u_ref, w_ref, o_ref, scr_ref, *, dil):
    y = _dot(u_ref[...], w_ref[...])
    nchunk, tm, _ = scr_ref.shape
    for c in range(nchunk):
        scr_ref[c] = y[:, c * LANE:(c + 1) * LANE]
        for r in range(dil):
            o_ref[r, :, c * LANE:(c + 1) * LANE] = (
                scr_ref[c, pl.ds(r, tm // dil, stride=dil), :].astype(o_ref.dtype))


def mix_dil(u, w_mix_in, layer, p_idx, tm=1024):
    t, d = u.shape
    dil = C_PATTERNS[p_idx][1]
    tm = _pick(t, tm, 8 * dil)
    return pl.pallas_call(
        functools.partial(_mix_dil_kernel, dil=dil),
        grid=(t // tm, 3),
        in_specs=[pl.BlockSpec((tm, d), lambda i, j: (i, 0)),
                  _wspec(layer, (d, C_OUT_W), lambda i, j: (0, BLK_CQ + p_idx + N_PAT * j))],
        out_specs=pl.BlockSpec((dil, tm // dil, C_OUT_W), lambda i, j: (0, i, j)),
        out_shape=jax.ShapeDtypeStruct((dil, t // dil, 3 * C_OUT_W), BF16),
        scratch_shapes=[pltpu.VMEM((C_OUT_W // LANE, tm, LANE), F32)],
        compiler_params=_params(2),
        name=f"mix_dil_p{p_idx}",
    )(u, w_mix_in)


def _attn_a_kernel(q_ref, k_ref, v_ref, *rest):
    o_ref = rest[-1]
    k = k_ref[...]
    v = v_ref[...]
    for g in range(A_GROUPS):
        sl = slice(g * HEAD, (g + 1) * HEAD)
        s = _dot_nt(q_ref[:, sl], k)
        p = jnp.exp2(s - jnp.max(s, axis=-1, keepdims=True))
        l = jnp.sum(p, axis=-1, keepdims=True)
        o = _dot(p.astype(BF16), v) * (1.0 / l)
        o_ref[:, sl] = o.astype(o_ref.dtype)


def attn_a(qk, nat, group, prev_out, tq=256):
    row0, nseq, s = group
    t = qk.shape[0]
    tq = _pick(s, tq, 8)
    nq = s // tq
    rb0, sb0 = row0 // tq, row0 // s
    gw = A_GROUPS * HEAD
    in_specs = [pl.BlockSpec((tq, gw), lambda b, h, i: (rb0 + b * nq + i, h)),
                pl.BlockSpec((s, HEAD), lambda b, h, i: (sb0 + b, A_Q_HEADS + h)),
                pl.BlockSpec((s, HEAD), lambda b, h, i: (sb0 + b, NAT_V * (C_OUT_W // HEAD) + h))]
    args = [qk, qk, nat]
    aliases = {}
    if prev_out is not None:
        in_specs.append(pl.BlockSpec(memory_space=pl.ANY))
        args.append(prev_out)
        aliases = {3: 0}
    return pl.pallas_call(
        _attn_a_kernel,
        grid=(nseq, A_KV_HEADS, nq),
        in_specs=in_specs,
        out_specs=pl.BlockSpec((tq, gw), lambda b, h, i: (rb0 + b * nq + i, h)),
        out_shape=jax.ShapeDtypeStruct((t, A_Q_W), BF16),
        input_output_aliases=aliases,
        compiler_params=_params(3),
        name="attn_a",
    )(*args)


RADIX = 8
DFT_TM = 256


def _chan_dft_kernel(z_ref, cs_ref, zc_ref, zs_ref, scr_ref):
    cs = cs_ref[...]
    rows = zc_ref.shape[1]
    for g in range(B_GROUPS):
        sl = slice(g * HEAD, (g + 1) * HEAD)
        y = _dot(z_ref[:, sl], cs) * (HEAD ** -0.5)
        scr_ref[2 * g] = y[:, :HEAD]
        scr_ref[2 * g + 1] = y[:, HEAD:]
        for b in range(RADIX):
            zc_ref[b, :, sl] = scr_ref[2 * g, pl.ds(b, rows, stride=RADIX), :].astype(zc_ref.dtype)
            zs_ref[b, :, sl] = scr_ref[2 * g + 1, pl.ds(b, rows, stride=RADIX), :].astype(zs_ref.dtype)


def chan_dft(nat, cs128, tr=512):
    t = nat.shape[0]
    tr = _pick(t, tr, 16 * RADIX)
    out = pl.BlockSpec((RADIX, tr // RADIX, B_W), lambda i: (0, i, 0))
    return pl.pallas_call(
        _chan_dft_kernel,
        grid=(t // tr,),
        in_specs=[pl.BlockSpec((tr, B_W), lambda i: (i, 0)),
                  pl.BlockSpec((HEAD, 2 * HEAD), lambda i: (0, 0))],
        out_specs=[out, out],
        out_shape=[jax.ShapeDtypeStruct((RADIX, t // RADIX, B_W), BF16)] * 2,
        scratch_shapes=[pltpu.VMEM((2 * B_GROUPS, tr, LANE), F32)],
        compiler_params=_params(1),
        name="chan_dft",
    )(nat, cs128)


def _outer_dft_real(tre, tim, k2):
    sums = {}
    for b in range(RADIX):
        ang = 2.0 * math.pi * ((k2 * b) % RADIX) / RADIX
        for coef, tile in ((math.cos(ang), tre[b]), (math.sin(ang), tim[b])):
            mag = round(abs(coef), 9)
            if mag == 0.0:
                continue
            term = tile if coef > 0 else -tile
            sums[mag] = term if mag not in sums else sums[mag] + term
    return sum(v if mag == 1.0 else v * mag for mag, v in sums.items())


def _seq_dft_kernel(m_ref, zc_ref, zs_ref, *rest, scale):
    o_ref = rest[-1]
    a = zc_ref.shape[1]
    tm = m_ref.shape[1] // 2
    tre, tim = [], []
    for b in range(RADIX):
        t = _dot(m_ref[b, :, :a], zc_ref[b]) + _dot(m_ref[b, :, a:], zs_ref[b])
        tre.append(t[:tm])
        tim.append(t[tm:])
    for k2 in range(RADIX):
        o_ref[k2] = (_outer_dft_real(tre, tim, k2) * scale).astype(o_ref.dtype)


def seq_dft(zc, zs, tables, group, prev_out, tn=256):
    row0, nseq, s = group
    t = zc.shape[1] * RADIX
    a = s // RADIX
    ni, tm2 = tables.shape[1], tables.shape[2]
    tm = tm2 // 2
    tn = _pick(B_W, tn)
    nj = B_W // tn
    sb0 = row0 // s
    zspec = pl.BlockSpec((RADIX, a, tn), lambda i, b, j: (0, sb0 + b, j))
    in_specs = [pl.BlockSpec((RADIX, None, tm2, 2 * a), lambda i, b, j: (0, i, 0, 0)), zspec, zspec]
    args = [tables, zc, zs]
    aliases = {}
    out_shape = (t // a, ni, tm, B_W)
    if prev_out is not None:
        in_specs.append(pl.BlockSpec(memory_space=pl.ANY))
        args.append(prev_out.reshape(out_shape))
        aliases = {3: 0}
    return pl.pallas_call(
        functools.partial(_seq_dft_kernel, scale=s ** -0.5),
        grid=(ni, nseq, nj),
        in_specs=in_specs,
        out_specs=pl.BlockSpec((RADIX, None, tm, tn), lambda i, b, j: (sb0 + b, i, 0, j)),
        out_shape=jax.ShapeDtypeStruct(out_shape, BF16),
        input_output_aliases=aliases,
        compiler_params=_params(3),
        name="seq_dft",
    )(*args).reshape(t, B_W)


def _dft_tables(s):
    a = s // RADIX
    tm = min(a, DFT_TM)
    k1 = jnp.arange(a, dtype=jnp.int32)
    n = RADIX * jnp.arange(a, dtype=jnp.int32)[None, :] + jnp.arange(RADIX, dtype=jnp.int32)[:, None]
    ang = ((k1[None, :, None] * n[:, None, :]) % s).astype(F32) * (2.0 * math.pi / s)
    c, sn = jnp.cos(ang), jnp.sin(ang)
    tile = lambda m: m.reshape(RADIX, a // tm, tm, 2 * a)
    re = tile(jnp.concatenate([c, -sn], axis=-1))
    im = tile(jnp.concatenate([-sn, -c], axis=-1))
    return jnp.concatenate([re, im], axis=2).astype(BF16)


def _dil_kernel(q_ref, kp_ref, km_ref, kn_ref, vp_ref, vm_ref, vn_ref, o_ref, l_ref, *,
                dil, half, rows, n_p, sub_p, sub_s, slopes):
    i = pl.program_id(1)
    start = i * rows
    in_p = start < n_p
    seq_len = jnp.where(in_p, sub_p, sub_s)
    local = jnp.where(in_p, start % sub_p, (start - n_p) % sub_s)
    qi = lax.broadcasted_iota(jnp.int32, (HEAD, 2 * HEAD), 0)
    ci = lax.broadcasted_iota(jnp.int32, (HEAD, 2 * HEAD), 1)
    rel = ci - HALO - qi
    band = jnp.abs(rel) <= half
    dist = (jnp.abs(rel) * dil).astype(F32)
    nsub = rows // HEAD

    def window(main_ref, prev_ref, next_ref, r, s, sl):
        lo, hi = s * HEAD - HALO, (s + 1) * HEAD + HALO
        parts = [prev_ref[r, :, sl]] if lo < 0 else []
        parts.append(main_ref[r, max(lo, 0):min(hi, rows), sl])
        if hi > rows:
            parts.append(next_ref[r, :, sl])
        return parts[0] if len(parts) == 1 else jnp.concatenate(parts, axis=0)

    for s in range(nsub):
        kpos = local + s * HEAD - HALO + ci
        valid = band & (kpos >= 0) & (kpos < seq_len)
        qs = slice(s * HEAD, (s + 1) * HEAD)
        for h in range(C_PER):
            sl = slice(h * HEAD, (h + 1) * HEAD)
            bias = jnp.where(valid, -slopes[h] * dist, NEG)
            for r in range(q_ref.shape[0]):
                sc = _dot_nt(q_ref[r, qs, sl], window(km_ref, kp_ref, kn_ref, r, s, sl))
                sc = sc * SCALE + bias
                m = jnp.max(sc, axis=-1, keepdims=True)
                e = jnp.exp(sc - m)
                den = jnp.sum(e, axis=-1, keepdims=True)
                p = (e * (1.0 / den)).astype(BF16)
                o_ref[r, qs, sl] = _dot(p, window(vm_ref, vp_ref, vn_ref, r, s, sl))
                l_ref[r, qs, sl] = jnp.broadcast_to(m + jnp.log(den), (HEAD, HEAD))


def dil_attn(qkv, col0, p_idx, groups, slopes, rows_per_step=512):
    window, dil = C_PATTERNS[p_idx]
    half = window // (2 * dil)
    assert half <= HALO
    (_, bp, sp), (_, bs, ss) = groups
    n = qkv.shape[1]
    n_p, sub_p, sub_s = bp * sp // dil, sp // dil, ss // dil
    assert sub_p % HEAD == 0 and sub_s % HEAD == 0
    rows = _pick(math.gcd(sub_p, sub_s), rows_per_step)
    per = rows // HALO
    nhalo = n // HALO
    nres = math.gcd(dil, max(1, rows_per_step // rows))

    main = lambda c: pl.BlockSpec((nres, rows, C_OUT_W), lambda r, i: (r, i, col0 + c))
    prev = lambda c: pl.BlockSpec((nres, HALO, C_OUT_W),
                                  lambda r, i: (r, jnp.maximum(i * per - 1, 0), col0 + c))
    nxt = lambda c: pl.BlockSpec((nres, HALO, C_OUT_W),
                                 lambda r, i: (r, jnp.minimum((i + 1) * per, nhalo - 1), col0 + c))
    out_spec = pl.BlockSpec((nres, rows, C_OUT_W), lambda r, i: (r, i, 0))
    out_sds = jax.ShapeDtypeStruct((dil, n, C_OUT_W), F32)
    return pl.pallas_call(
        functools.partial(_dil_kernel, dil=dil, half=half, rows=rows, n_p=n_p, sub_p=sub_p, sub_s=sub_s,
                          slopes=tuple(slopes[p_idx * C_PER:(p_idx + 1) * C_PER])),
        grid=(dil // nres, n // rows),
        in_specs=[main(0), prev(1), main(1), nxt(1), prev(2), main(2), nxt(2)],
        out_specs=[out_spec, out_spec],
        out_shape=[out_sds, out_sds],
        compiler_params=_params(2),
        name=f"dil_attn_p{p_idx}",
    )(*([qkv] * 7))


def _dil_combine_kernel(*refs):
    srcs, out_ref, scr = refs[:2 * N_PAT], refs[2 * N_PAT], list(refs[2 * N_PAT + 1:])
    vals = []
    for src in srcs:
        dil = src.shape[0]
        if dil == 1:
            vals.append(src[0])
            continue
        nat = scr.pop(0)
        nchunk = nat.shape[0]
        for r in range(dil):
            for c in range(nchunk):
                nat[c, pl.ds(r, src.shape[1], stride=dil), :] = src[r, :, c * LANE:(c + 1) * LANE]
        vals.append(jnp.concatenate([nat[c] for c in range(nchunk)], axis=-1))
    outs, lses = vals[:N_PAT], vals[N_PAT:]
    m = functools.reduce(jnp.maximum, lses)
    es = [jnp.exp(l - m) for l in lses]
    num = sum(e * o for e, o in zip(es, outs))
    out_ref[...] = (num / sum(es)).astype(out_ref.dtype)


def dil_combine(outs, lses, tr=512):
    t = outs[0].shape[0] * outs[0].shape[1]
    tr = _pick(t, tr, 8 * max(d for _, d in C_PATTERNS))
    spec = lambda a: pl.BlockSpec((a.shape[0], tr // a.shape[0], C_OUT_W), lambda i: (0, i, 0))
    n_scr = sum(2 for a in outs if a.shape[0] > 1)
    return pl.pallas_call(
        _dil_combine_kernel,
        grid=(t // tr,),
        in_specs=[spec(a) for a in outs + lses],
        out_specs=pl.BlockSpec((tr, C_OUT_W), lambda i: (i, 0)),
        out_shape=jax.ShapeDtypeStruct((t, C_OUT_W), BF16),
        scratch_shapes=[pltpu.VMEM((C_OUT_W // LANE, tr, LANE), F32)] * n_scr,
        compiler_params=_params(1),
        name="dil_combine",
    )(*outs, *lses)


def _xattn_block_kernel(o_ref, x_ref, gmix_ref, gpre_ref, wq_ref, kv_ref, wo_ref, gpost_ref, gnext_ref,
                        xo_ref, h_ref):
    xn = x_ref[...] + _rms(o_ref[...].astype(F32), gmix_ref[...])
    q = _dot(_rms(xn, gpre_ref[...]).astype(BF16), wq_ref[...]).astype(BF16)
    heads = []
    for h in range(X_HEADS):
        sl = slice(h * HEAD, (h + 1) * HEAD)
        s = _dot_nt(q[:, sl], kv_ref[:, sl]) * SCALE
        p = jnp.exp(s - jnp.max(s, axis=-1, keepdims=True))
        l = jnp.sum(p, axis=-1, keepdims=True)
        o = _dot(p.astype(BF16), kv_ref[:, X_W + h * HEAD:X_W + (h + 1) * HEAD]) * (1.0 / l)
        heads.append(o.astype(BF16))
    xn = xn + _rms(_dot(jnp.concatenate(heads, axis=-1), wo_ref[...]), gpost_ref[...])
    xo_ref[...] = xn
    h_ref[...] = _rms(xn, gnext_ref[...]).astype(h_ref.dtype)


def xattn_block(o, x, g_mix_post, g_pre, w_xq, kv, w_xo, g_post, g_next, layer, groups, n_mem, tm=256):
    t, d = x.shape
    (_, bp, sp), (_, _, ss) = groups
    tm = _pick(math.gcd(sp, ss), tm, 8)
    n_p = bp * sp // tm

    def seq_of(i):
        return jnp.where(i < n_p, i // (sp // tm), bp + (i - n_p) // (ss // tm))

    row = pl.BlockSpec((tm, d), lambda i: (i, 0))
    vec = _wspec(layer, (1, d), lambda i: (0, 0))
    g3 = lambda g: g.reshape(g.shape[0], 1, d)
    return pl.pallas_call(
        _xattn_block_kernel,
        grid=(t // tm,),
        in_specs=[row, row, vec, vec,
                  _wspec(layer, (d, X_W), lambda i: (0, 0)),
                  pl.BlockSpec((n_mem, 2 * X_W), lambda i: (seq_of(i), 0)),
                  _wspec(layer, (X_W, d), lambda i: (0, 0)), vec, vec],
        out_specs=[row, row],
        out_shape=[jax.ShapeDtypeStruct((t, d), F32), jax.ShapeDtypeStruct((t, d), BF16)],
        compiler_params=_params(1),
        name="xattn_block",
    )(o, x, g3(g_mix_post), g3(g_pre), w_xq, kv, w_xo, g3(g_post), g3(g_next))


def _rope_tables(seq_len):
    t = jnp.arange(seq_len)
    row = (t // GRID_W).astype(F32)
    col = (t % GRID_W).astype(F32)
    n_pairs = HEAD // 4
    inv_freq = ROPE_BASE ** (-jnp.arange(n_pairs, dtype=F32) / n_pairs)
    ang = jnp.concatenate([row[:, None] * inv_freq[None, :], col[:, None] * inv_freq[None, :]], axis=-1)
    cos = jnp.repeat(jnp.cos(ang), 2, axis=-1)
    sin = jnp.repeat(jnp.sin(ang), 2, axis=-1)
    even = (jnp.arange(HEAD) % 2 == 0)[None, :]
    return cos, jnp.where(even, -sin, 0.0), jnp.where(even, 0.0, sin)


def _alibi_slopes():
    h = np.arange(1, C_HEADS + 1, dtype=np.float32)
    return [float(v) for v in np.float32(2.0) ** (-np.float32(ALIBI_MAX_EXP) * h / np.float32(C_HEADS))]


def kernel(x_prompt, x_sample, mem_prompt, mem_sample, ffn1_pre_norm, ffn1_w_in, ffn1_w_out, ffn1_post_norm, mix_pre_norm, w_mix_in, a_q_norm, a_k_norm, w_a_proj, w_b_proj, w_c_proj, w_branch_gate, b_branch_gate, w_mix_out, mix_post_norm, xattn_pre_norm, mem_norm, w_xq, w_xkv, w_xo, xattn_post_norm, ffn2_pre_norm, ffn2_w_in, ffn2_w_out, ffn2_post_norm):
    bp, sp, d = x_prompt.shape
    bs, ss, _ = x_sample.shape
    n_mem = mem_prompt.shape[1]
    depth = ffn1_w_in.shape[0]
    tp = bp * sp
    groups = ((0, bp, sp), (tp, bs, ss))
    assert tp % ss == 0 and sp % GRID_W == 0 and ss % GRID_W == 0

    x = (x_prompt.reshape(tp, d), x_sample.reshape(bs * ss, d))
    mem = jnp.concatenate([mem_prompt.reshape(bp * n_mem, d), mem_sample.reshape(bs * n_mem, d)], axis=0)

    rope = _rope_tables(max(sp, ss))
    dft = {s: _dft_tables(s) for s in {sp, ss}}
    kc = np.arange(HEAD)
    ang128 = 2.0 * np.pi * ((kc[:, None] * kc[None, :]) % HEAD) / HEAD
    cs128 = jnp.asarray(np.concatenate([np.cos(ang128), np.sin(ang128)], axis=1), BF16)
    slopes = _alibi_slopes()
    head_scale = jnp.concatenate([jnp.full((A_Q_HEADS, HEAD), SCALE * math.log2(math.e), F32),
                                  jnp.ones((A_KV_HEADS, HEAD), F32)], axis=0)
    qk_gains = jnp.concatenate([jnp.repeat(a_q_norm[:, None, :], A_Q_HEADS, axis=1),
                                jnp.repeat(a_k_norm[:, None, :], A_KV_HEADS, axis=1)], axis=1)

    bf = lambda w: w.astype(BF16)
    ffn1_w_out, ffn2_w_out = bf(ffn1_w_out), bf(ffn2_w_out)
    w_mix_in, w_branch_gate, w_mix_out = bf(w_mix_in), bf(w_branch_gate), bf(w_mix_out)
    w_a_proj, w_b_proj, w_c_proj = bf(w_a_proj), bf(w_b_proj), bf(w_c_proj)
    w_xq, w_xkv, w_xo = bf(w_xq), bf(w_xkv), bf(w_xo)

    nat_block = lambda j: jnp.where(j < NAT_V, BLK_B + j,
                                    jnp.where(j == NAT_V, BLK_V, BLK_CQ + N_PAT * (j - NAT_C)))

    h = rms_rows(x, ffn1_pre_norm, 0)
    for l in range(depth):
        act = swiglu_in(h, ffn1_w_in, l)
        o = matmul(act, ffn1_w_out, l, BF16, tm=512, tn=512, name="ffn_out")
        x, u = resid_norm(o, x, ffn1_post_norm, l, mix_pre_norm, l, 0.5, tp)

        qk = mix_qk(u, w_mix_in, l, qk_gains, head_scale, rope, groups)
        nat = matmul(u, w_mix_in, l, BF16, tn=C_OUT_W, n_out=NAT_W, col_block=nat_block, name="mix_nat")
        y_a = y_b = None
        for g in groups:
            y_a = attn_a(qk, nat, g, y_a)
        zc, zs = chan_dft(nat, cs128)
        for g in groups:
            y_b = seq_dft(zc, zs, dft[g[2]], g, y_b)
        outs, lses = [], []
        for p_idx, (_, dil) in enumerate(C_PATTERNS):
            if dil == 1:
                qkv, col0 = nat.reshape(1, *nat.shape), NAT_C
            else:
                qkv, col0 = mix_dil(u, w_mix_in, l, p_idx), 0
            o_p, l_p = dil_attn(qkv, col0, p_idx, groups, slopes)
            outs.append(o_p)
            lses.append(l_p)
        y_c = dil_combine(outs, lses)
        merged = merge_branches(u, y_a, y_b, y_c, w_branch_gate, b_branch_gate,
                                w_a_proj, w_b_proj, w_c_proj, l)
        o = matmul(merged, w_mix_out, l, BF16, name="mix_out")
        kv = matmul(rms_rows(mem, mem_norm, l), w_xkv, l, BF16, name="xattn_kv")
        x, h = xattn_block(o, x, mix_post_norm, xattn_pre_norm, w_xq, kv, w_xo, xattn_post_norm,
                           ffn2_pre_norm, l, groups, n_mem)

        act = swiglu_in(h, ffn2_w_in, l)
        o = matmul(act, ffn2_w_out, l, BF16, tm=512, tn=512, name="ffn_out")
        if l + 1 < depth:
            x, h = resid_norm(o, x, ffn2_post_norm, l, ffn1_pre_norm, l + 1, 0.5, tp)
        else:
            y_p, y_s = resid_norm(o, x, ffn2_post_norm, l, None, None, 0.5, tp)

    return y_p.reshape(bp, sp, d), y_s.reshape(bs, ss, d)
```

```python
import functools
import math

import numpy as np
import jax
import jax.numpy as jnp
from jax import lax
from jax.experimental import pallas as pl
from jax.experimental.pallas import tpu as pltpu

F32 = jnp.float32
BF16 = jnp.bfloat16

HEAD = 128
GRID_W = 64
EPS = 1e-6
ROPE_BASE = 10000.0
A_Q_HEADS = 12
A_KV_HEADS = 4
A_GROUPS = A_Q_HEADS // A_KV_HEADS
B_GROUPS = 20
C_PATTERNS = ((128, 1), (512, 4), (2048, 16))
C_PER = 4
C_HEADS = C_PER * len(C_PATTERNS)
ALIBI_MAX_EXP = 8.0
X_HEADS = 4
N_BRANCH = 3

A_Q_W = A_Q_HEADS * HEAD
A_KV_W = A_KV_HEADS * HEAD
QK_W = A_Q_W + A_KV_W
B_W = B_GROUPS * HEAD
C_W = C_HEADS * HEAD
C_OUT_W = C_PER * HEAD
X_W = X_HEADS * HEAD
MIX_W = A_Q_W + 2 * A_KV_W + B_W + 3 * C_W
BLK_V = QK_W // C_OUT_W
BLK_CQ = (QK_W + A_KV_W + B_W) // C_OUT_W
N_PAT = len(C_PATTERNS)
BLK_B = (QK_W + A_KV_W) // C_OUT_W
NAT_W = B_W + A_KV_W + 3 * C_OUT_W
NAT_V = B_W // C_OUT_W
NAT_C = NAT_V + 1
SCALE = HEAD ** -0.5
NEG = -1e30
HALO = HEAD // 2

VMEM_LIMIT_V7X = 56 * 1024 * 1024
LANE = 128


def _params(n_axes):
    return pltpu.CompilerParams(dimension_semantics=("arbitrary",) * n_axes,
                                vmem_limit_bytes=VMEM_LIMIT_V7X)


def _pick(n, pref, unit=LANE):
    if n <= pref:
        return n
    best = None
    for t in range(unit, pref + 1, unit):
        if n % t == 0:
            best = t
    assert best is not None, (n, pref)
    return best


def _wspec(layer, shape, imap):
    return pl.BlockSpec((None,) + shape, lambda *g: (layer,) + tuple(imap(*g)))


def _rms(x, g):
    return x * lax.rsqrt(jnp.mean(x * x, axis=-1, keepdims=True) + EPS) * g


def _sigmoid(x):
    return 1.0 / (1.0 + jnp.exp(-x))


def _dot(a, b):
    return jnp.dot(a, b, preferred_element_type=F32)


def _dot_nt(a, b):
    return lax.dot_general(a, b, (((1,), (1,)), ((), ())), preferred_element_type=F32)


def _row_inputs(x, tr):
    if not isinstance(x, tuple):
        return [x], [pl.BlockSpec((tr, x.shape[1]), lambda i: (i, 0))], None
    xp, xs = x
    n_p = xp.shape[0] // tr
    d = xp.shape[1]
    return ([xp, xs],
            [pl.BlockSpec((tr, d), lambda i: (jnp.minimum(i, n_p - 1), 0)),
             pl.BlockSpec((tr, d), lambda i: (jnp.maximum(i - n_p, 0), 0))], n_p)


def _load_rows(x_refs, n_p):
    if len(x_refs) == 1:
        return x_refs[0][...]
    return jnp.where(pl.program_id(0) < n_p, x_refs[0][...], x_refs[1][...])


def _rms_kernel(*refs, n_p):
    x_refs, g_ref, o_ref = refs[:-2], refs[-2], refs[-1]
    o_ref[...] = _rms(_load_rows(x_refs, n_p), g_ref[...]).astype(o_ref.dtype)


def rms_rows(x, g, layer, tr=256):
    pieces = x if isinstance(x, tuple) else (x,)
    t = sum(a.shape[0] for a in pieces)
    d = g.shape[-1]
    tr = _pick(math.gcd(*[a.shape[0] for a in pieces]), tr, 8)
    arrs, specs, n_p = _row_inputs(x, tr)
    return pl.pallas_call(
        functools.partial(_rms_kernel, n_p=n_p),
        grid=(t // tr,),
        in_specs=specs + [_wspec(layer, (1, d), lambda i: (0, 0))],
        out_specs=pl.BlockSpec((tr, d), lambda i: (i, 0)),
        out_shape=jax.ShapeDtypeStruct((t, d), BF16),
        compiler_params=_params(1),
        name="rms_rows",
    )(*arrs, g.reshape(g.shape[0], 1, d))


def _resid_kernel(*refs, coef, n_x, n_p, has_next):
    o_ref, x_refs, gp_ref, rest = refs[0], refs[1:1 + n_x], refs[1 + n_x], refs[2 + n_x:]
    xn = _load_rows(x_refs, n_p) + coef * _rms(o_ref[...].astype(F32), gp_ref[...])
    if has_next:
        gn_ref, xo_ref, h_ref = rest
        xo_ref[...] = xn
        h_ref[...] = _rms(xn, gn_ref[...]).astype(h_ref.dtype)
    else:
        yp_ref, ys_ref = rest
        i = pl.program_id(0)

        @pl.when(i < n_p)
        def _():
            yp_ref[...] = xn

        @pl.when(i >= n_p)
        def _():
            ys_ref[...] = xn


def resid_norm(o, x, g_post, layer, g_next, next_layer, coef, rows_p, tr=256):
    t, d = o.shape
    tr = _pick(math.gcd(rows_p, t - rows_p), tr, 8)
    arrs, specs, n_p = _row_inputs(x, tr)
    row = pl.BlockSpec((tr, d), lambda i: (i, 0))
    vec = lambda lyr: _wspec(lyr, (1, d), lambda i: (0, 0))
    g3 = lambda g: g.reshape(g.shape[0], 1, d)
    if g_next is None:
        n_p = rows_p // tr
        return pl.pallas_call(
            functools.partial(_resid_kernel, coef=coef, n_x=len(arrs), n_p=n_p, has_next=False),
            grid=(t // tr,),
            in_specs=[row] + specs + [vec(layer)],
            out_specs=[pl.BlockSpec((tr, d), lambda i: (jnp.minimum(i, n_p - 1), 0)),
                       pl.BlockSpec((tr, d), lambda i: (jnp.maximum(i - n_p, 0), 0))],
            out_shape=[jax.ShapeDtypeStruct((rows_p, d), F32), jax.ShapeDtypeStruct((t - rows_p, d), F32)],
            compiler_params=_params(1),
            name="resid_last",
        )(o, *arrs, g3(g_post))
    return pl.pallas_call(
        functools.partial(_resid_kernel, coef=coef, n_x=len(arrs), n_p=n_p, has_next=True),
        grid=(t // tr,),
        in_specs=[row] + specs + [vec(layer), vec(next_layer)],
        out_specs=[row, row],
        out_shape=[jax.ShapeDtypeStruct((t, d), F32), jax.ShapeDtypeStruct((t, d), BF16)],
        compiler_params=_params(1),
        name="resid_norm",
    )(o, *arrs, g3(g_post), g3(g_next))


BF16_ROWS = 16


def _with_riders(body, n_in, n_out, n_riders):
    def kern(*refs):
        ins, r_in = refs[:n_in], refs[n_in:n_in + n_riders]
        outs = refs[n_in + n_riders:n_in + n_riders + n_out]
        r_out = refs[n_in + n_riders + n_out:n_in + 2 * n_riders + n_out]
        body(*ins, *outs, *refs[n_in + 2 * n_riders + n_out:])
        for src, dst in zip(r_in, r_out):
            dst[...] = src[...].astype(dst.dtype)
    return kern


def _rider_specs(riders, layer, grid):
    n_steps = grid[0] * grid[1]
    in_specs, out_specs, out_shapes = [], [], []
    for w in riders:
        _, k, n = w.shape
        assert k % (n_steps * BF16_ROWS) == 0, (w.shape, grid)
        slab = (k // n_steps, n)
        step = lambda i, j: (i * grid[1] + j, 0)
        in_specs.append(_wspec(layer, slab, step))
        out_specs.append(pl.BlockSpec(slab, step))
        out_shapes.append(jax.ShapeDtypeStruct((k, n), BF16))
    return in_specs, out_specs, out_shapes


def _mm_kernel(x_ref, w_ref, o_ref):
    o_ref[...] = _dot(x_ref[...], w_ref[...]).astype(o_ref.dtype)


def matmul(x, w, layer, out_dtype, tm=1024, tn=1024, n_out=None, col_block=None, name="matmul",
           riders=(), rider_layer=None):
    m, k = x.shape
    n = w.shape[2] if n_out is None else n_out
    tm = _pick(m, tm, 8)
    tn = _pick(n, tn)
    col_block = col_block or (lambda j: j)
    grid = (m // tm, n // tn)
    r_in, r_out, r_shapes = _rider_specs(riders, rider_layer, grid)
    out = pl.pallas_call(
        _with_riders(_mm_kernel, 2, 1, len(riders)),
        grid=grid,
        in_specs=[pl.BlockSpec((tm, k), lambda i, j: (i, 0)),
                  _wspec(layer, (k, tn), lambda i, j: (0, col_block(j)))] + r_in,
        out_specs=[pl.BlockSpec((tm, tn), lambda i, j: (i, j))] + r_out,
        out_shape=[jax.ShapeDtypeStruct((m, n), out_dtype)] + r_shapes,
        compiler_params=_params(2),
        name=name,
    )(x, w, *riders)
    return out if riders else out[0]


def _swiglu_kernel(x_ref, wg_ref, wu_ref, o_ref):
    x = x_ref[...]
    g = _dot(x, wg_ref[...].astype(x.dtype))
    u = _dot(x, wu_ref[...].astype(x.dtype))
    o_ref[...] = (g * _sigmoid(g) * u).astype(o_ref.dtype)


def swiglu_in(h, w_in, layer, tm=2048, tn=256, riders=()):
    t, d = h.shape
    f = w_in.shape[2] // 2
    tm = _pick(t, tm, 8)
    tn = _pick(f, tn)
    nj = f // tn
    grid = (t // tm, nj)
    r_in, r_out, r_shapes = _rider_specs(riders, layer, grid)
    return pl.pallas_call(
        _with_riders(_swiglu_kernel, 3, 1, len(riders)),
        grid=grid,
        in_specs=[pl.BlockSpec((tm, d), lambda i, j: (i, 0)),
                  _wspec(layer, (d, tn), lambda i, j: (0, j)),
                  _wspec(layer, (d, tn), lambda i, j: (0, nj + j))] + r_in,
        out_specs=[pl.BlockSpec((tm, tn), lambda i, j: (i, j))] + r_out,
        out_shape=[jax.ShapeDtypeStruct((t, f), BF16)] + r_shapes,
        compiler_params=_params(2),
        name="swiglu_in",
    )(h, w_in, w_in, *riders)


def _merge_kernel(u_ref, a_ref, b_ref, c_ref, wg0_ref, wg1_ref, wg2_ref, bg0_ref, bg1_ref, bg2_ref,
                  wa_ref, wb_ref, wc_ref, o_ref):
    u = u_ref[...]
    acc = _sigmoid(_dot(u, wg0_ref[...]) + bg0_ref[...]) * _dot(a_ref[...], wa_ref[...])
    acc += _sigmoid(_dot(u, wg1_ref[...]) + bg1_ref[...]) * _dot(b_ref[...], wb_ref[...])
    acc += _sigmoid(_dot(u, wg2_ref[...]) + bg2_ref[...]) * _dot(c_ref[...], wc_ref[...])
    o_ref[...] = acc.astype(o_ref.dtype)


def merge_branches(u, a, b, c, w_gate, gate_layer, b_gate, w_a, w_b, w_c, layer, tm=512, tn=256):
    t, d = u.shape
    tm = _pick(t, tm, 8)
    tn = _pick(d, tn)
    nj = d // tn
    act = lambda width: pl.BlockSpec((tm, width), lambda i, j: (i, 0))
    gate_w = lambda br: _wspec(gate_layer, (d, tn), lambda i, j: (0, br * nj + j))
    gate_b = lambda br: _wspec(layer, (1, tn), lambda i, j: (0, br * nj + j))
    proj_w = lambda width: _wspec(layer, (width, tn), lambda i, j: (0, j))
    bg = b_gate.reshape(b_gate.shape[0], 1, N_BRANCH * d)
    return pl.pallas_call(
        _merge_kernel,
        grid=(t // tm, nj),
        in_specs=[act(d), act(A_Q_W), act(B_W), act(C_OUT_W),
                  gate_w(0), gate_w(1), gate_w(2), gate_b(0), gate_b(1), gate_b(2),
                  proj_w(A_Q_W), proj_w(B_W), proj_w(C_OUT_W)],
        out_specs=pl.BlockSpec((tm, tn), lambda i, j: (i, j)),
        out_shape=jax.ShapeDtypeStruct((t, d), BF16),
        compiler_params=_params(2),
        name="merge_branches",
    )(u, a, b, c, w_gate, w_gate, w_gate, bg, bg, bg, w_a, w_b, w_c)


def _mix_qk_kernel(u_ref, w_ref, g_ref, sc_ref, cs_ref, sa_ref, sb_ref, o_ref):
    y = _dot(u_ref[...], w_ref[...])
    cs, sa, sb = cs_ref[...], sa_ref[...], sb_ref[...]
    for h in range(o_ref.shape[1] // HEAD):
        sl = slice(h * HEAD, (h + 1) * HEAD)
        x = _rms(y[:, sl], g_ref[h:h + 1, :])
        r = x * cs + pltpu.roll(x, HEAD - 1, 1) * sa + pltpu.roll(x, 1, 1) * sb
        o_ref[:, sl] = (r * sc_ref[h:h + 1, :]).astype(o_ref.dtype)


def mix_qk(u, w_mix_in, layer, gains, gain_layer, scales, rope, groups, tm=1024, tn=1024):
    t, d = u.shape
    (_, bp, sp), (_, _, ss) = groups
    tm = _pick(math.gcd(sp, ss), tm, 8)
    tn = _pick(QK_W, tn)
    n_p = bp * sp // tm
    hpt = tn // HEAD

    def pos_block(i):
        return jnp.where(i < n_p, i % (sp // tm), (i - n_p) % (ss // tm))

    tab = pl.BlockSpec((tm, HEAD), lambda i, j: (pos_block(i), 0))
    return pl.pallas_call(
        _mix_qk_kernel,
        grid=(t // tm, QK_W // tn),
        in_specs=[pl.BlockSpec((tm, d), lambda i, j: (i, 0)),
                  _wspec(layer, (d, tn), lambda i, j: (0, j)),
                  _wspec(gain_layer, (hpt, HEAD), lambda i, j: (j, 0)),
                  pl.BlockSpec((hpt, HEAD), lambda i, j: (j, 0)),
                  tab, tab, tab],
        out_specs=pl.BlockSpec((tm, tn), lambda i, j: (i, j)),
        out_shape=jax.ShapeDtypeStruct((t, QK_W), BF16),
        compiler_params=_params(2),
        name="mix_qk",
    )(u, w_mix_in, gains, scales, *rope)


def _mix_dil_kernel(u_ref, w_ref, o_ref, scr_ref, *, dil):
    y = _dot(u_ref[...], w_ref[...])
    nchunk, tm, _ = scr_ref.shape
    for c in range(nchunk):
        scr_ref[c] = y[:, c * LANE:(c + 1) * LANE]
        for r in range(dil):
            o_ref[r, :, c * LANE:(c + 1) * LANE] = (
                scr_ref[c, pl.ds(r, tm // dil, stride=dil), :].astype(o_ref.dtype))


def mix_dil(u, w_mix_in, layer, p_idx, tm=1024):
    t, d = u.shape
    dil = C_PATTERNS[p_idx][1]
    tm = _pick(t, tm, 8 * dil)
    return pl.pallas_call(
        functools.partial(_mix_dil_kernel, dil=dil),
        grid=(t // tm, 3),
        in_specs=[pl.BlockSpec((tm, d), lambda i, j: (i, 0)),
                  _wspec(layer, (d, C_OUT_W), lambda i, j: (0, BLK_CQ + p_idx + N_PAT * j))],
        out_specs=pl.BlockSpec((dil, tm // dil, C_OUT_W), lambda i, j: (0, i, j)),
        out_shape=jax.ShapeDtypeStruct((dil, t // dil, 3 * C_OUT_W), BF16),
        scratch_shapes=[pltpu.VMEM((C_OUT_W // LANE, tm, LANE), F32)],
        compiler_params=_params(2),
        name=f"mix_dil_p{p_idx}",
    )(u, w_mix_in)


def _attn_a_kernel(q_ref, k_ref, v_ref, *rest):
    o_ref = rest[-1]
    k = k_ref[...]
    v = v_ref[...]
    for g in range(A_GROUPS):
        sl = slice(g * HEAD, (g + 1) * HEAD)
        s = _dot_nt(q_ref[:, sl], k)
        p = jnp.exp2(s - jnp.max(s, axis=-1, keepdims=True))
        l = jnp.sum(p, axis=-1, keepdims=True)
        o = _dot(p.astype(BF16), v) * (1.0 / l)
        o_ref[:, sl] = o.astype(o_ref.dtype)


def attn_a(qk, nat, group, prev_out, tq=256):
    row0, nseq, s = group
    t = qk.shape[0]
    tq = _pick(s, tq, 8)
    nq = s // tq
    rb0, sb0 = row0 // tq, row0 // s
    gw = A_GROUPS * HEAD
    in_specs = [pl.BlockSpec((tq, gw), lambda b, h, i: (rb0 + b * nq + i, h)),
                pl.BlockSpec((s, HEAD), lambda b, h, i: (sb0 + b, A_Q_HEADS + h)),
                pl.BlockSpec((s, HEAD), lambda b, h, i: (sb0 + b, NAT_V * (C_OUT_W // HEAD) + h))]
    args = [qk, qk, nat]
    aliases = {}
    if prev_out is not None:
        in_specs.append(pl.BlockSpec(memory_space=pl.ANY))
        args.append(prev_out)
        aliases = {3: 0}
    return pl.pallas_call(
        _attn_a_kernel,
        grid=(nseq, A_KV_HEADS, nq),
        in_specs=in_specs,
        out_specs=pl.BlockSpec((tq, gw), lambda b, h, i: (rb0 + b * nq + i, h)),
        out_shape=jax.ShapeDtypeStruct((t, A_Q_W), BF16),
        input_output_aliases=aliases,
        compiler_params=_params(3),
        name="attn_a",
    )(*args)


RADIX = 8
DFT_TM = 256


def _chan_dft_kernel(z_ref, cs_ref, zc_ref, zs_ref, scr_ref):
    cs = cs_ref[...]
    rows = zc_ref.shape[1]
    for g in range(B_GROUPS):
        sl = slice(g * HEAD, (g + 1) * HEAD)
        y = _dot(z_ref[:, sl], cs) * (HEAD ** -0.5)
        scr_ref[2 * g] = y[:, :HEAD]
        scr_ref[2 * g + 1] = y[:, HEAD:]
        for b in range(RADIX):
            zc_ref[b, :, sl] = scr_ref[2 * g, pl.ds(b, rows, stride=RADIX), :].astype(zc_ref.dtype)
            zs_ref[b, :, sl] = scr_ref[2 * g + 1, pl.ds(b, rows, stride=RADIX), :].astype(zs_ref.dtype)


def chan_dft(nat, cs128, tr=512):
    t = nat.shape[0]
    tr = _pick(t, tr, 16 * RADIX)
    out = pl.BlockSpec((RADIX, tr // RADIX, B_W), lambda i: (0, i, 0))
    return pl.pallas_call(
        _chan_dft_kernel,
        grid=(t // tr,),
        in_specs=[pl.BlockSpec((tr, B_W), lambda i: (i, 0)),
                  pl.BlockSpec((HEAD, 2 * HEAD), lambda i: (0, 0))],
        out_specs=[out, out],
        out_shape=[jax.ShapeDtypeStruct((RADIX, t // RADIX, B_W), BF16)] * 2,
        scratch_shapes=[pltpu.VMEM((2 * B_GROUPS, tr, LANE), F32)],
        compiler_params=_params(1),
        name="chan_dft",
    )(nat, cs128)


def _outer_dft_real(tre, tim, k2):
    sums = {}
    for b in range(RADIX):
        ang = 2.0 * math.pi * ((k2 * b) % RADIX) / RADIX
        for coef, tile in ((math.cos(ang), tre[b]), (math.sin(ang), tim[b])):
            mag = round(abs(coef), 9)
            if mag == 0.0:
                continue
            term = tile if coef > 0 else -tile
            sums[mag] = term if mag not in sums else sums[mag] + term
    return sum(v if mag == 1.0 else v * mag for mag, v in sums.items())


def _seq_dft_kernel(m_ref, zc_ref, zs_ref, *rest, scale):
    o_ref = rest[-1]
    a = zc_ref.shape[1]
    tm = m_ref.shape[1] // 2
    tre, tim = [], []
    for b in range(RADIX):
        t = _dot(m_ref[b, :, :a], zc_ref[b]) + _dot(m_ref[b, :, a:], zs_ref[b])
        tre.append(t[:tm])
        tim.append(t[tm:])
    for k2 in range(RADIX):
        o_ref[k2] = (_outer_dft_real(tre, tim, k2) * scale).astype(o_ref.dtype)


def seq_dft(zc, zs, tables, group, prev_out, tn=256):
    row0, nseq, s = group
    t = zc.shape[1] * RADIX
    a = s // RADIX
    ni, tm2 = tables.shape[1], tables.shape[2]
    tm = tm2 // 2
    tn = _pick(B_W, tn)
    nj = B_W // tn
    sb0 = row0 // s
    zspec = pl.BlockSpec((RADIX, a, tn), lambda i, b, j: (0, sb0 + b, j))
    in_specs = [pl.BlockSpec((RADIX, None, tm2, 2 * a), lambda i, b, j: (0, i, 0, 0)), zspec, zspec]
    args = [tables, zc, zs]
    aliases = {}
    out_shape = (t // a, ni, tm, B_W)
    if prev_out is not None:
        in_specs.append(pl.BlockSpec(memory_space=pl.ANY))
        args.append(prev_out.reshape(out_shape))
        aliases = {3: 0}
    return pl.pallas_call(
        functools.partial(_seq_dft_kernel, scale=s ** -0.5),
        grid=(ni, nseq, nj),
        in_specs=in_specs,
        out_specs=pl.BlockSpec((RADIX, None, tm, tn), lambda i, b, j: (sb0 + b, i, 0, j)),
        out_shape=jax.ShapeDtypeStruct(out_shape, BF16),
        input_output_aliases=aliases,
        compiler_params=_params(3),
        name="seq_dft",
    )(*args).reshape(t, B_W)


def _dft_tables(s):
    a = s // RADIX
    tm = min(a, DFT_TM)
    k1 = jnp.arange(a, dtype=jnp.int32)
    n = RADIX * jnp.arange(a, dtype=jnp.int32)[None, :] + jnp.arange(RADIX, dtype=jnp.int32)[:, None]
    ang = ((k1[None, :, None] * n[:, None, :]) % s).astype(F32) * (2.0 * math.pi / s)
    c, sn = jnp.cos(ang), jnp.sin(ang)
    tile = lambda m: m.reshape(RADIX, a // tm, tm, 2 * a)
    re = tile(jnp.concatenate([c, -sn], axis=-1))
    im = tile(jnp.concatenate([-sn, -c], axis=-1))
    return jnp.concatenate([re, im], axis=2).astype(BF16)


def _dil_kernel(q_ref, kp_ref, km_ref, kn_ref, vp_ref, vm_ref, vn_ref, o_ref, l_ref, *,
                dil, half, rows, n_p, sub_p, sub_s, slopes):
    i = pl.program_id(1)
    start = i * rows
    in_p = start < n_p
    seq_len = jnp.where(in_p, sub_p, sub_s)
    local = jnp.where(in_p, start % sub_p, (start - n_p) % sub_s)
    qi = lax.broadcasted_iota(jnp.int32, (HEAD, 2 * HEAD), 0)
    ci = lax.broadcasted_iota(jnp.int32, (HEAD, 2 * HEAD), 1)
    rel = ci - HALO - qi
    band = jnp.abs(rel) <= half
    dist = (jnp.abs(rel) * dil).astype(F32)
    nsub = rows // HEAD

    def window(main_ref, prev_ref, next_ref, r, s, sl):
        lo, hi = s * HEAD - HALO, (s + 1) * HEAD + HALO
        parts = [prev_ref[r, :, sl]] if lo < 0 else []
        parts.append(main_ref[r, max(lo, 0):min(hi, rows), sl])
        if hi > rows:
            parts.append(next_ref[r, :, sl])
        return parts[0] if len(parts) == 1 else jnp.concatenate(parts, axis=0)

    for s in range(nsub):
        kpos = local + s * HEAD - HALO + ci
        valid = band & (kpos >= 0) & (kpos < seq_len)
        qs = slice(s * HEAD, (s + 1) * HEAD)
        for h in range(C_PER):
            sl = slice(h * HEAD, (h + 1) * HEAD)
            bias = jnp.where(valid, -slopes[h] * dist, NEG)
            for r in range(q_ref.shape[0]):
                sc = _dot_nt(q_ref[r, qs, sl], window(km_ref, kp_ref, kn_ref, r, s, sl))
                sc = sc * SCALE + bias
                m = jnp.max(sc, axis=-1, keepdims=True)
                e = jnp.exp(sc - m)
                den = jnp.sum(e, axis=-1, keepdims=True)
                p = (e * (1.0 / den)).astype(BF16)
                o_ref[r, qs, sl] = _dot(p, window(vm_ref, vp_ref, vn_ref, r, s, sl))
                l_ref[r, qs, sl] = jnp.broadcast_to(m + jnp.log(den), (HEAD, HEAD))


def dil_attn(qkv, col0, p_idx, groups, slopes, rows_per_step=512):
    window, dil = C_PATTERNS[p_idx]
    half = window // (2 * dil)
    assert half <= HALO
    (_, bp, sp), (_, bs, ss) = groups
    n = qkv.shape[1]
    n_p, sub_p, sub_s = bp * sp // dil, sp // dil, ss // dil
    assert sub_p % HEAD == 0 and sub_s % HEAD == 0
    rows = _pick(math.gcd(sub_p, sub_s), rows_per_step)
    per = rows // HALO
    nhalo = n // HALO
    nres = math.gcd(dil, max(1, rows_per_step // rows))

    main = lambda c: pl.BlockSpec((nres, rows, C_OUT_W), lambda r, i: (r, i, col0 + c))
    prev = lambda c: pl.BlockSpec((nres, HALO, C_OUT_W),
                                  lambda r, i: (r, jnp.maximum(i * per - 1, 0), col0 + c))
    nxt = lambda c: pl.BlockSpec((nres, HALO, C_OUT_W),
                                 lambda r, i: (r, jnp.minimum((i + 1) * per, nhalo - 1), col0 + c))
    out_spec = pl.BlockSpec((nres, rows, C_OUT_W), lambda r, i: (r, i, 0))
    out_sds = jax.ShapeDtypeStruct((dil, n, C_OUT_W), F32)
    return pl.pallas_call(
        functools.partial(_dil_kernel, dil=dil, half=half, rows=rows, n_p=n_p, sub_p=sub_p, sub_s=sub_s,
                          slopes=tuple(slopes[p_idx * C_PER:(p_idx + 1) * C_PER])),
        grid=(dil // nres, n // rows),
        in_specs=[main(0), prev(1), main(1), nxt(1), prev(2), main(2), nxt(2)],
        out_specs=[out_spec, out_spec],
        out_shape=[out_sds, out_sds],
        compiler_params=_params(2),
        name=f"dil_attn_p{p_idx}",
    )(*([qkv] * 7))


def _dil_combine_kernel(*refs):
    srcs, out_ref, scr = refs[:2 * N_PAT], refs[2 * N_PAT], list(refs[2 * N_PAT + 1:])
    vals = []
    for src in srcs:
        dil = src.shape[0]
        if dil == 1:
            vals.append(src[0])
            continue
        nat = scr.pop(0)
        nchunk = nat.shape[0]
        for r in range(dil):
            for c in range(nchunk):
                nat[c, pl.ds(r, src.shape[1], stride=dil), :] = src[r, :, c * LANE:(c + 1) * LANE]
        vals.append(jnp.concatenate([nat[c] for c in range(nchunk)], axis=-1))
    outs, lses = vals[:N_PAT], vals[N_PAT:]
    m = functools.reduce(jnp.maximum, lses)
    es = [jnp.exp(l - m) for l in lses]
    num = sum(e * o for e, o in zip(es, outs))
    out_ref[...] = (num / sum(es)).astype(out_ref.dtype)


def dil_combine(outs, lses, tr=512):
    t = outs[0].shape[0] * outs[0].shape[1]
    tr = _pick(t, tr, 8 * max(d for _, d in C_PATTERNS))
    spec = lambda a: pl.BlockSpec((a.shape[0], tr // a.shape[0], C_OUT_W), lambda i: (0, i, 0))
    n_scr = sum(2 for a in outs if a.shape[0] > 1)
    return pl.pallas_call(
        _dil_combine_kernel,
        grid=(t // tr,),
        in_specs=[spec(a) for a in outs + lses],
        out_specs=pl.BlockSpec((tr, C_OUT_W), lambda i: (i, 0)),
        out_shape=jax.ShapeDtypeStruct((t, C_OUT_W), BF16),
        scratch_shapes=[pltpu.VMEM((C_OUT_W // LANE, tr, LANE), F32)] * n_scr,
        compiler_params=_params(1),
        name="dil_combine",
    )(*outs, *lses)


def _xattn_block_kernel(o_ref, x_ref, gmix_ref, gpre_ref, wq_ref, kv_ref, wo_ref, gpost_ref, gnext_ref,
                        xo_ref, h_ref):
    xn = x_ref[...] + _rms(o_ref[...].astype(F32), gmix_ref[...])
    q = _dot(_rms(xn, gpre_ref[...]).astype(BF16), wq_ref[...]).astype(BF16)
    heads = []
    for h in range(X_HEADS):
        sl = slice(h * HEAD, (h + 1) * HEAD)
        s = _dot_nt(q[:, sl], kv_ref[:, sl]) * SCALE
        p = jnp.exp(s - jnp.max(s, axis=-1, keepdims=True))
        l = jnp.sum(p, axis=-1, keepdims=True)
        o = _dot(p.astype(BF16), kv_ref[:, X_W + h * HEAD:X_W + (h + 1) * HEAD]) * (1.0 / l)
        heads.append(o.astype(BF16))
    xn = xn + _rms(_dot(jnp.concatenate(heads, axis=-1), wo_ref[...]), gpost_ref[...])
    xo_ref[...] = xn
    h_ref[...] = _rms(xn, gnext_ref[...]).astype(h_ref.dtype)


def xattn_block(o, x, g_mix_post, g_pre, w_xq, kv, w_xo, g_post, g_next, layer, groups, n_mem, tm=256):
    t, d = x.shape
    (_, bp, sp), (_, _, ss) = groups
    tm = _pick(math.gcd(sp, ss), tm, 8)
    n_p = bp * sp // tm

    def seq_of(i):
        return jnp.where(i < n_p, i // (sp // tm), bp + (i - n_p) // (ss // tm))

    row = pl.BlockSpec((tm, d), lambda i: (i, 0))
    vec = _wspec(layer, (1, d), lambda i: (0, 0))
    g3 = lambda g: g.reshape(g.shape[0], 1, d)
    return pl.pallas_call(
        _xattn_block_kernel,
        grid=(t // tm,),
        in_specs=[row, row, vec, vec,
                  _wspec(layer, (d, X_W), lambda i: (0, 0)),
                  pl.BlockSpec((n_mem, 2 * X_W), lambda i: (seq_of(i), 0)),
                  _wspec(layer, (X_W, d), lambda i: (0, 0)), vec, vec],
        out_specs=[row, row],
        out_shape=[jax.ShapeDtypeStruct((t, d), F32), jax.ShapeDtypeStruct((t, d), BF16)],
        compiler_params=_params(1),
        name="xattn_block",
    )(o, x, g3(g_mix_post), g3(g_pre), w_xq, kv, w_xo, g3(g_post), g3(g_next))


def _rope_tables(seq_len):
    t = jnp.arange(seq_len)
    row = (t // GRID_W).astype(F32)
    col = (t % GRID_W).astype(F32)
    n_pairs = HEAD // 4
    inv_freq = ROPE_BASE ** (-jnp.arange(n_pairs, dtype=F32) / n_pairs)
    ang = jnp.concatenate([row[:, None] * inv_freq[None, :], col[:, None] * inv_freq[None, :]], axis=-1)
    cos = jnp.repeat(jnp.cos(ang), 2, axis=-1)
    sin = jnp.repeat(jnp.sin(ang), 2, axis=-1)
    even = (jnp.arange(HEAD) % 2 == 0)[None, :]
    return cos, jnp.where(even, -sin, 0.0), jnp.where(even, 0.0, sin)


def _alibi_slopes():
    h = np.arange(1, C_HEADS + 1, dtype=np.float32)
    return [float(v) for v in np.float32(2.0) ** (-np.float32(ALIBI_MAX_EXP) * h / np.float32(C_HEADS))]


def kernel(x_prompt, x_sample, mem_prompt, mem_sample, ffn1_pre_norm, ffn1_w_in, ffn1_w_out, ffn1_post_norm, mix_pre_norm, w_mix_in, a_q_norm, a_k_norm, w_a_proj, w_b_proj, w_c_proj, w_branch_gate, b_branch_gate, w_mix_out, mix_post_norm, xattn_pre_norm, mem_norm, w_xq, w_xkv, w_xo, xattn_post_norm, ffn2_pre_norm, ffn2_w_in, ffn2_w_out, ffn2_post_norm):
    bp, sp, d = x_prompt.shape
    bs, ss, _ = x_sample.shape
    n_mem = mem_prompt.shape[1]
    depth = ffn1_w_in.shape[0]
    tp = bp * sp
    groups = ((0, bp, sp), (tp, bs, ss))
    assert tp % ss == 0 and sp % GRID_W == 0 and ss % GRID_W == 0

    x = (x_prompt.reshape(tp, d), x_sample.reshape(bs * ss, d))
    mem = jnp.concatenate([mem_prompt.reshape(bp * n_mem, d), mem_sample.reshape(bs * n_mem, d)], axis=0)

    rope = _rope_tables(max(sp, ss))
    dft = {s: _dft_tables(s) for s in {sp, ss}}
    kc = np.arange(HEAD)
    ang128 = 2.0 * np.pi * ((kc[:, None] * kc[None, :]) % HEAD) / HEAD
    cs128 = jnp.asarray(np.concatenate([np.cos(ang128), np.sin(ang128)], axis=1), BF16)
    slopes = _alibi_slopes()
    head_scale = jnp.concatenate([jnp.full((A_Q_HEADS, HEAD), SCALE * math.log2(math.e), F32),
                                  jnp.ones((A_KV_HEADS, HEAD), F32)], axis=0)
    qk_gains = jnp.concatenate([jnp.repeat(a_q_norm[:, None, :], A_Q_HEADS, axis=1),
                                jnp.repeat(a_k_norm[:, None, :], A_KV_HEADS, axis=1)], axis=1)

    bf = lambda w: w.astype(BF16)
    w_mix_out = bf(w_mix_out)
    w_a_proj, w_b_proj, w_c_proj = bf(w_a_proj), bf(w_b_proj), bf(w_c_proj)
    w_xq, w_xkv, w_xo = bf(w_xq), bf(w_xkv), bf(w_xo)

    nat_block = lambda j: jnp.where(j < NAT_V, BLK_B + j,
                                    jnp.where(j == NAT_V, BLK_V, BLK_CQ + N_PAT * (j - NAT_C)))

    h = rms_rows(x, ffn1_pre_norm, 0)
    for l in range(depth):
        act, w_out = swiglu_in(h, ffn1_w_in, l, riders=[ffn1_w_out])
        o, w_gate, w_min = matmul(act, w_out[None], 0, BF16, tm=512, tn=512, name="ffn_out",
                                  riders=[w_branch_gate, w_mix_in], rider_layer=l)
        w_gate, w_min = w_gate[None], w_min[None]
        x, u = resid_norm(o, x, ffn1_post_norm, l, mix_pre_norm, l, 0.5, tp)

        qk = mix_qk(u, w_min, 0, qk_gains, l, head_scale, rope, groups)
        nat = matmul(u, w_min, 0, BF16, tn=C_OUT_W, n_out=NAT_W, col_block=nat_block, name="mix_nat")
        y_a = y_b = None
        for g in groups:
            y_a = attn_a(qk, nat, g, y_a)
        zc, zs = chan_dft(nat, cs128)
        for g in groups:
            y_b = seq_dft(zc, zs, dft[g[2]], g, y_b)
        outs, lses = [], []
        for p_idx, (_, dil) in enumerate(C_PATTERNS):
            if dil == 1:
                qkv, col0 = nat.reshape(1, *nat.shape), NAT_C
            else:
                qkv, col0 = mix_dil(u, w_min, 0, p_idx), 0
            o_p, l_p = dil_attn(qkv, col0, p_idx, groups, slopes)
            outs.append(o_p)
            lses.append(l_p)
        y_c = dil_combine(outs, lses)
        merged = merge_branches(u, y_a, y_b, y_c, w_gate, 0, b_branch_gate,
                                w_a_proj, w_b_proj, w_c_proj, l)
        o = matmul(merged, w_mix_out, l, BF16, name="mix_out")
        kv = matmul(rms_rows(mem, mem_norm, l), w_xkv, l, BF16, name="xattn_kv")
        x, h = xattn_block(o, x, mix_post_norm, xattn_pre_norm, w_xq, kv, w_xo, xattn_post_norm,
                           ffn2_pre_norm, l, groups, n_mem)

        act, w_out = swiglu_in(h, ffn2_w_in, l, riders=[ffn2_w_out])
        o = matmul(act, w_out[None], 0, BF16, tm=512, tn=512, name="ffn_out")
        if l + 1 < depth:
            x, h = resid_norm(o, x, ffn2_post_norm, l, ffn1_pre_norm, l + 1, 0.5, tp)
        else:
            y_p, y_s = resid_norm(o, x, ffn2_post_norm, l, None, None, 0.5, tp)

    return y_p.reshape(bp, sp, d), y_s.reshape(bs, ss, d)
```

```python
import functools
import math

import numpy as np
import jax
import jax.numpy as jnp
from jax import lax
from jax.experimental import pallas as pl
from jax.experimental.pallas import tpu as pltpu

F32 = jnp.float32
BF16 = jnp.bfloat16

HEAD = 128
GRID_W = 64
EPS = 1e-6
ROPE_BASE = 10000.0
A_Q_HEADS = 12
A_KV_HEADS = 4
A_GROUPS = A_Q_HEADS // A_KV_HEADS
B_GROUPS = 20
C_PATTERNS = ((128, 1), (512, 4), (2048, 16))
C_PER = 4
C_HEADS = C_PER * len(C_PATTERNS)
ALIBI_MAX_EXP = 8.0
X_HEADS = 4
N_BRANCH = 3

A_Q_W = A_Q_HEADS * HEAD
A_KV_W = A_KV_HEADS * HEAD
QK_W = A_Q_W + A_KV_W
B_W = B_GROUPS * HEAD
C_W = C_HEADS * HEAD
C_OUT_W = C_PER * HEAD
X_W = X_HEADS * HEAD
MIX_W = A_Q_W + 2 * A_KV_W + B_W + 3 * C_W
BLK_V = QK_W // C_OUT_W
BLK_CQ = (QK_W + A_KV_W + B_W) // C_OUT_W
N_PAT = len(C_PATTERNS)
BLK_B = (QK_W + A_KV_W) // C_OUT_W
NAT_W = B_W + A_KV_W + 3 * C_OUT_W
NAT_V = B_W // C_OUT_W
NAT_C = NAT_V + 1
SCALE = HEAD ** -0.5
NEG = -1e30
HALO = HEAD // 2

VMEM_LIMIT_V7X = 56 * 1024 * 1024
LANE = 128


def _params(n_axes):
    return pltpu.CompilerParams(dimension_semantics=("arbitrary",) * n_axes,
                                vmem_limit_bytes=VMEM_LIMIT_V7X)


def _pick(n, pref, unit=LANE):
    if n <= pref:
        return n
    best = None
    for t in range(unit, pref + 1, unit):
        if n % t == 0:
            best = t
    assert best is not None, (n, pref)
    return best


def _wspec(layer, shape, imap):
    return pl.BlockSpec((None,) + shape, lambda *g: (layer,) + tuple(imap(*g)))


def _rms(x, g):
    return x * lax.rsqrt(jnp.mean(x * x, axis=-1, keepdims=True) + EPS) * g


def _sigmoid(x):
    return 1.0 / (1.0 + jnp.exp(-x))


def _dot(a, b):
    return jnp.dot(a, b, preferred_element_type=F32)


def _dot_nt(a, b):
    return lax.dot_general(a, b, (((1,), (1,)), ((), ())), preferred_element_type=F32)


def _row_inputs(x, tr):
    if not isinstance(x, tuple):
        return [x], [pl.BlockSpec((tr, x.shape[1]), lambda i: (i, 0))], None
    xp, xs = x
    n_p = xp.shape[0] // tr
    d = xp.shape[1]
    return ([xp, xs],
            [pl.BlockSpec((tr, d), lambda i: (jnp.minimum(i, n_p - 1), 0)),
             pl.BlockSpec((tr, d), lambda i: (jnp.maximum(i - n_p, 0), 0))], n_p)


def _load_rows(x_refs, n_p):
    if len(x_refs) == 1:
        return x_refs[0][...]
    return jnp.where(pl.program_id(0) < n_p, x_refs[0][...], x_refs[1][...])


def _rms_kernel(*refs, n_p):
    x_refs, g_ref, o_ref = refs[:-2], refs[-2], refs[-1]
    o_ref[...] = _rms(_load_rows(x_refs, n_p), g_ref[...]).astype(o_ref.dtype)


def rms_rows(x, g, layer, tr=256):
    pieces = x if isinstance(x, tuple) else (x,)
    t = sum(a.shape[0] for a in pieces)
    d = g.shape[-1]
    tr = _pick(math.gcd(*[a.shape[0] for a in pieces]), tr, 8)
    arrs, specs, n_p = _row_inputs(x, tr)
    return pl.pallas_call(
        functools.partial(_rms_kernel, n_p=n_p),
        grid=(t // tr,),
        in_specs=specs + [_wspec(layer, (1, d), lambda i: (0, 0))],
        out_specs=pl.BlockSpec((tr, d), lambda i: (i, 0)),
        out_shape=jax.ShapeDtypeStruct((t, d), BF16),
        compiler_params=_params(1),
        name="rms_rows",
    )(*arrs, g.reshape(g.shape[0], 1, d))


def _resid_kernel(*refs, coef, n_x, n_p, has_next):
    o_ref, x_refs, gp_ref, rest = refs[0], refs[1:1 + n_x], refs[1 + n_x], refs[2 + n_x:]
    xn = _load_rows(x_refs, n_p) + coef * _rms(o_ref[...].astype(F32), gp_ref[...])
    if has_next:
        gn_ref, xo_ref, h_ref = rest
        xo_ref[...] = xn
        h_ref[...] = _rms(xn, gn_ref[...]).astype(h_ref.dtype)
    else:
        yp_ref, ys_ref = rest
        i = pl.program_id(0)

        @pl.when(i < n_p)
        def _():
            yp_ref[...] = xn

        @pl.when(i >= n_p)
        def _():
            ys_ref[...] = xn


def resid_norm(o, x, g_post, layer, g_next, next_layer, coef, rows_p, tr=256):
    t, d = o.shape
    tr = _pick(math.gcd(rows_p, t - rows_p), tr, 8)
    arrs, specs, n_p = _row_inputs(x, tr)
    row = pl.BlockSpec((tr, d), lambda i: (i, 0))
    vec = lambda lyr: _wspec(lyr, (1, d), lambda i: (0, 0))
    g3 = lambda g: g.reshape(g.shape[0], 1, d)
    if g_next is None:
        n_p = rows_p // tr
        return pl.pallas_call(
            functools.partial(_resid_kernel, coef=coef, n_x=len(arrs), n_p=n_p, has_next=False),
            grid=(t // tr,),
            in_specs=[row] + specs + [vec(layer)],
            out_specs=[pl.BlockSpec((tr, d), lambda i: (jnp.minimum(i, n_p - 1), 0)),
                       pl.BlockSpec((tr, d), lambda i: (jnp.maximum(i - n_p, 0), 0))],
            out_shape=[jax.ShapeDtypeStruct((rows_p, d), F32), jax.ShapeDtypeStruct((t - rows_p, d), F32)],
            compiler_params=_params(1),
            name="resid_last",
        )(o, *arrs, g3(g_post))
    return pl.pallas_call(
        functools.partial(_resid_kernel, coef=coef, n_x=len(arrs), n_p=n_p, has_next=True),
        grid=(t // tr,),
        in_specs=[row] + specs + [vec(layer), vec(next_layer)],
        out_specs=[row, row],
        out_shape=[jax.ShapeDtypeStruct((t, d), F32), jax.ShapeDtypeStruct((t, d), BF16)],
        compiler_params=_params(1),
        name="resid_norm",
    )(o, *arrs, g3(g_post), g3(g_next))


BF16_ROWS = 16


def _with_riders(body, n_in, n_out, n_riders):
    def kern(*refs):
        ins, r_in = refs[:n_in], refs[n_in:n_in + n_riders]
        outs = refs[n_in + n_riders:n_in + n_riders + n_out]
        r_out = refs[n_in + n_riders + n_out:n_in + 2 * n_riders + n_out]
        body(*ins, *outs, *refs[n_in + 2 * n_riders + n_out:])
        for src, dst in zip(r_in, r_out):
            if len(dst.shape) == 2:
                dst[...] = src[...].astype(dst.dtype)
            else:
                ct = dst.shape[2]
                for c in range(dst.shape[0]):
                    dst[c] = src[:, c * ct:(c + 1) * ct].astype(dst.dtype)
    return kern


def _rider_specs(riders, layer, grid):
    n_steps = grid[0] * grid[1]
    in_specs, out_specs, out_shapes = [], [], []
    for w, ct in riders:
        _, k, n = w.shape
        assert k % (n_steps * BF16_ROWS) == 0, (w.shape, grid)
        rows = k // n_steps
        in_specs.append(_wspec(layer, (rows, n), lambda i, j: (i * grid[1] + j, 0)))
        if ct is None:
            out_specs.append(pl.BlockSpec((rows, n), lambda i, j: (i * grid[1] + j, 0)))
            out_shapes.append(jax.ShapeDtypeStruct((k, n), BF16))
        else:
            out_specs.append(pl.BlockSpec((n // ct, rows, ct), lambda i, j: (0, i * grid[1] + j, 0)))
            out_shapes.append(jax.ShapeDtypeStruct((n // ct, k, ct), BF16))
    return in_specs, out_specs, out_shapes


def _mm_kernel(x_ref, w_ref, o_ref):
    o_ref[...] = _dot(x_ref[...], w_ref[...]).astype(o_ref.dtype)


def matmul(x, w, layer, out_dtype, tm=1024, tn=1024, n_out=None, col_block=None, name="matmul",
           riders=(), rider_layer=None):
    m, k = x.shape
    if layer is None:
        tn, n = w.shape[2], w.shape[0] * w.shape[2]
        w_spec = pl.BlockSpec((None, k, tn), lambda i, j: (j, 0, 0))
    else:
        n = w.shape[2] if n_out is None else n_out
        tn = _pick(n, tn)
        col_block = col_block or (lambda j: j)
        w_spec = _wspec(layer, (k, tn), lambda i, j: (0, col_block(j)))
    tm = _pick(m, tm, 8)
    grid = (m // tm, n // tn)
    r_in, r_out, r_shapes = _rider_specs(riders, rider_layer, grid)
    out = pl.pallas_call(
        _with_riders(_mm_kernel, 2, 1, len(riders)),
        grid=grid,
        in_specs=[pl.BlockSpec((tm, k), lambda i, j: (i, 0)), w_spec] + r_in,
        out_specs=[pl.BlockSpec((tm, tn), lambda i, j: (i, j))] + r_out,
        out_shape=[jax.ShapeDtypeStruct((m, n), out_dtype)] + r_shapes,
        compiler_params=_params(2),
        name=name,
    )(x, w, *[r for r, _ in riders])
    return out if riders else out[0]


def _swiglu_kernel(x_ref, wg_ref, wu_ref, o_ref):
    x = x_ref[...]
    g = _dot(x, wg_ref[...].astype(x.dtype))
    u = _dot(x, wu_ref[...].astype(x.dtype))
    o_ref[...] = (g * _sigmoid(g) * u).astype(o_ref.dtype)


def swiglu_in(h, w_in, layer, tm=2048, tn=256, riders=()):
    t, d = h.shape
    f = w_in.shape[2] // 2
    tm = _pick(t, tm, 8)
    tn = _pick(f, tn)
    nj = f // tn
    grid = (t // tm, nj)
    r_in, r_out, r_shapes = _rider_specs(riders, layer, grid)
    return pl.pallas_call(
        _with_riders(_swiglu_kernel, 3, 1, len(riders)),
        grid=grid,
        in_specs=[pl.BlockSpec((tm, d), lambda i, j: (i, 0)),
                  _wspec(layer, (d, tn), lambda i, j: (0, j)),
                  _wspec(layer, (d, tn), lambda i, j: (0, nj + j))] + r_in,
        out_specs=[pl.BlockSpec((tm, tn), lambda i, j: (i, j))] + r_out,
        out_shape=[jax.ShapeDtypeStruct((t, f), BF16)] + r_shapes,
        compiler_params=_params(2),
        name="swiglu_in",
    )(h, w_in, w_in, *[r for r, _ in riders])


def _merge_kernel(u_ref, a_ref, b_ref, c_ref, wg0_ref, wg1_ref, wg2_ref, bg0_ref, bg1_ref, bg2_ref,
                  wa_ref, wb_ref, wc_ref, o_ref):
    u = u_ref[...]
    acc = _sigmoid(_dot(u, wg0_ref[...]) + bg0_ref[...]) * _dot(a_ref[...], wa_ref[...])
    acc += _sigmoid(_dot(u, wg1_ref[...]) + bg1_ref[...]) * _dot(b_ref[...], wb_ref[...])
    acc += _sigmoid(_dot(u, wg2_ref[...]) + bg2_ref[...]) * _dot(c_ref[...], wc_ref[...])
    o_ref[...] = acc.astype(o_ref.dtype)


def merge_branches(u, a, b, c, w_gate, b_gate, w_a, w_b, w_c, layer, tm=512):
    t, d = u.shape
    tm = _pick(t, tm, 8)
    tn = w_gate.shape[2]
    nj = d // tn
    act = lambda width: pl.BlockSpec((tm, width), lambda i, j: (i, 0))
    gate_w = lambda br: pl.BlockSpec((None, d, tn), lambda i, j: (br * nj + j, 0, 0))
    gate_b = lambda br: _wspec(layer, (1, tn), lambda i, j: (0, br * nj + j))
    proj_w = lambda width: _wspec(layer, (width, tn), lambda i, j: (0, j))
    bg = b_gate.reshape(b_gate.shape[0], 1, N_BRANCH * d)
    return pl.pallas_call(
        _merge_kernel,
        grid=(t // tm, nj),
        in_specs=[act(d), act(A_Q_W), act(B_W), act(C_OUT_W),
                  gate_w(0), gate_w(1), gate_w(2), gate_b(0), gate_b(1), gate_b(2),
                  proj_w(A_Q_W), proj_w(B_W), proj_w(C_OUT_W)],
        out_specs=pl.BlockSpec((tm, tn), lambda i, j: (i, j)),
        out_shape=jax.ShapeDtypeStruct((t, d), BF16),
        compiler_params=_params(2),
        name="merge_branches",
    )(u, a, b, c, w_gate, w_gate, w_gate, bg, bg, bg, w_a, w_b, w_c)


def _mix_qk_kernel(u_ref, w_ref, g_ref, sc_ref, cs_ref, sa_ref, sb_ref, o_ref):
    y = _dot(u_ref[...], w_ref[...])
    cs, sa, sb = cs_ref[...], sa_ref[...], sb_ref[...]
    for h in range(o_ref.shape[1] // HEAD):
        sl = slice(h * HEAD, (h + 1) * HEAD)
        x = _rms(y[:, sl], g_ref[h:h + 1, :])
        r = x * cs + pltpu.roll(x, HEAD - 1, 1) * sa + pltpu.roll(x, 1, 1) * sb
        o_ref[:, sl] = (r * sc_ref[h:h + 1, :]).astype(o_ref.dtype)


def mix_qk(u, w_mix_in, layer, gains, gain_layer, scales, rope, groups, tm=1024, tn=1024):
    t, d = u.shape
    (_, bp, sp), (_, _, ss) = groups
    tm = _pick(math.gcd(sp, ss), tm, 8)
    tn = _pick(QK_W, tn)
    n_p = bp * sp // tm
    hpt = tn // HEAD

    def pos_block(i):
        return jnp.where(i < n_p, i % (sp // tm), (i - n_p) % (ss // tm))

    tab = pl.BlockSpec((tm, HEAD), lambda i, j: (pos_block(i), 0))
    return pl.pallas_call(
        _mix_qk_kernel,
        grid=(t // tm, QK_W // tn),
        in_specs=[pl.BlockSpec((tm, d), lambda i, j: (i, 0)),
                  _wspec(layer, (d, tn), lambda i, j: (0, j)),
                  _wspec(gain_layer, (hpt, HEAD), lambda i, j: (j, 0)),
                  pl.BlockSpec((hpt, HEAD), lambda i, j: (j, 0)),
                  tab, tab, tab],
        out_specs=pl.BlockSpec((tm, tn), lambda i, j: (i, j)),
        out_shape=jax.ShapeDtypeStruct((t, QK_W), BF16),
        compiler_params=_params(2),
        name="mix_qk",
    )(u, w_mix_in, gains, scales, *rope)


def _mix_dil_kernel(u_ref, w_ref, o_ref, scr_ref, *, dil):
    y = _dot(u_ref[...], w_ref[...])
    nchunk, tm, _ = scr_ref.shape
    for c in range(nchunk):
        scr_ref[c] = y[:, c * LANE:(c + 1) * LANE]
        for r in range(dil):
            o_ref[r, :, c * LANE:(c + 1) * LANE] = (
                scr_ref[c, pl.ds(r, tm // dil, stride=dil), :].astype(o_ref.dtype))


def mix_dil(u, w_mix_in, layer, p_idx, tm=1024):
    t, d = u.shape
    dil = C_PATTERNS[p_idx][1]
    tm = _pick(t, tm, 8 * dil)
    return pl.pallas_call(
        functools.partial(_mix_dil_kernel, dil=dil),
        grid=(t // tm, 3),
        in_specs=[pl.BlockSpec((tm, d), lambda i, j: (i, 0)),
                  _wspec(layer, (d, C_OUT_W), lambda i, j: (0, BLK_CQ + p_idx + N_PAT * j))],
        out_specs=pl.BlockSpec((dil, tm // dil, C_OUT_W), lambda i, j: (0, i, j)),
        out_shape=jax.ShapeDtypeStruct((dil, t // dil, 3 * C_OUT_W), BF16),
        scratch_shapes=[pltpu.VMEM((C_OUT_W // LANE, tm, LANE), F32)],
        compiler_params=_params(2),
        name=f"mix_dil_p{p_idx}",
    )(u, w_mix_in)


def _attn_a_kernel(q_ref, k_ref, v_ref, *rest):
    o_ref = rest[-1]
    k = k_ref[...]
    v = v_ref[...]
    for g in range(A_GROUPS):
        sl = slice(g * HEAD, (g + 1) * HEAD)
        s = _dot_nt(q_ref[:, sl], k)
        p = jnp.exp2(s - jnp.max(s, axis=-1, keepdims=True))
        l = jnp.sum(p, axis=-1, keepdims=True)
        o = _dot(p.astype(BF16), v) * (1.0 / l)
        o_ref[:, sl] = o.astype(o_ref.dtype)


def attn_a(qk, nat, group, prev_out, tq=256):
    row0, nseq, s = group
    t = qk.shape[0]
    tq = _pick(s, tq, 8)
    nq = s // tq
    rb0, sb0 = row0 // tq, row0 // s
    gw = A_GROUPS * HEAD
    in_specs = [pl.BlockSpec((tq, gw), lambda b, h, i: (rb0 + b * nq + i, h)),
                pl.BlockSpec((s, HEAD), lambda b, h, i: (sb0 + b, A_Q_HEADS + h)),
                pl.BlockSpec((s, HEAD), lambda b, h, i: (sb0 + b, NAT_V * (C_OUT_W // HEAD) + h))]
    args = [qk, qk, nat]
    aliases = {}
    if prev_out is not None:
        in_specs.append(pl.BlockSpec(memory_space=pl.ANY))
        args.append(prev_out)
        aliases = {3: 0}
    return pl.pallas_call(
        _attn_a_kernel,
        grid=(nseq, A_KV_HEADS, nq),
        in_specs=in_specs,
        out_specs=pl.BlockSpec((tq, gw), lambda b, h, i: (rb0 + b * nq + i, h)),
        out_shape=jax.ShapeDtypeStruct((t, A_Q_W), BF16),
        input_output_aliases=aliases,
        compiler_params=_params(3),
        name="attn_a",
    )(*args)


RADIX = 8
DFT_TM = 256


def _chan_dft_kernel(z_ref, cs_ref, zc_ref, zs_ref, scr_ref):
    cs = cs_ref[...]
    rows = zc_ref.shape[1]
    for g in range(B_GROUPS):
        sl = slice(g * HEAD, (g + 1) * HEAD)
        y = _dot(z_ref[:, sl], cs) * (HEAD ** -0.5)
        scr_ref[2 * g] = y[:, :HEAD]
        scr_ref[2 * g + 1] = y[:, HEAD:]
        for b in range(RADIX):
            zc_ref[b, :, sl] = scr_ref[2 * g, pl.ds(b, rows, stride=RADIX), :].astype(zc_ref.dtype)
            zs_ref[b, :, sl] = scr_ref[2 * g + 1, pl.ds(b, rows, stride=RADIX), :].astype(zs_ref.dtype)


def chan_dft(nat, cs128, tr=512):
    t = nat.shape[0]
    tr = _pick(t, tr, 16 * RADIX)
    out = pl.BlockSpec((RADIX, tr // RADIX, B_W), lambda i: (0, i, 0))
    return pl.pallas_call(
        _chan_dft_kernel,
        grid=(t // tr,),
        in_specs=[pl.BlockSpec((tr, B_W), lambda i: (i, 0)),
                  pl.BlockSpec((HEAD, 2 * HEAD), lambda i: (0, 0))],
        out_specs=[out, out],
        out_shape=[jax.ShapeDtypeStruct((RADIX, t // RADIX, B_W), BF16)] * 2,
        scratch_shapes=[pltpu.VMEM((2 * B_GROUPS, tr, LANE), F32)],
        compiler_params=_params(1),
        name="chan_dft",
    )(nat, cs128)


def _outer_dft_real(tre, tim, k2):
    sums = {}
    for b in range(RADIX):
        ang = 2.0 * math.pi * ((k2 * b) % RADIX) / RADIX
        for coef, tile in ((math.cos(ang), tre[b]), (math.sin(ang), tim[b])):
            mag = round(abs(coef), 9)
            if mag == 0.0:
                continue
            term = tile if coef > 0 else -tile
            sums[mag] = term if mag not in sums else sums[mag] + term
    return sum(v if mag == 1.0 else v * mag for mag, v in sums.items())


def _seq_dft_kernel(m_ref, zc_ref, zs_ref, *rest, scale):
    o_ref = rest[-1]
    a = zc_ref.shape[1]
    tm = m_ref.shape[1] // 2
    tre, tim = [], []
    for b in range(RADIX):
        t = _dot(m_ref[b, :, :a], zc_ref[b]) + _dot(m_ref[b, :, a:], zs_ref[b])
        tre.append(t[:tm])
        tim.append(t[tm:])
    for k2 in range(RADIX):
        o_ref[k2] = (_outer_dft_real(tre, tim, k2) * scale).astype(o_ref.dtype)


def seq_dft(zc, zs, tables, group, prev_out, tn=256):
    row0, nseq, s = group
    t = zc.shape[1] * RADIX
    a = s // RADIX
    ni, tm2 = tables.shape[1], tables.shape[2]
    tm = tm2 // 2
    tn = _pick(B_W, tn)
    nj = B_W // tn
    sb0 = row0 // s
    zspec = pl.BlockSpec((RADIX, a, tn), lambda i, b, j: (0, sb0 + b, j))
    in_specs = [pl.BlockSpec((RADIX, None, tm2, 2 * a), lambda i, b, j: (0, i, 0, 0)), zspec, zspec]
    args = [tables, zc, zs]
    aliases = {}
    out_shape = (t // a, ni, tm, B_W)
    if prev_out is not None:
        in_specs.append(pl.BlockSpec(memory_space=pl.ANY))
        args.append(prev_out.reshape(out_shape))
        aliases = {3: 0}
    return pl.pallas_call(
        functools.partial(_seq_dft_kernel, scale=s ** -0.5),
        grid=(ni, nseq, nj),
        in_specs=in_specs,
        out_specs=pl.BlockSpec((RADIX, None, tm, tn), lambda i, b, j: (sb0 + b, i, 0, j)),
        out_shape=jax.ShapeDtypeStruct(out_shape, BF16),
        input_output_aliases=aliases,
        compiler_params=_params(3),
        name="seq_dft",
    )(*args).reshape(t, B_W)


def _dft_tables(s):
    a = s // RADIX
    tm = min(a, DFT_TM)
    k1 = jnp.arange(a, dtype=jnp.int32)
    n = RADIX * jnp.arange(a, dtype=jnp.int32)[None, :] + jnp.arange(RADIX, dtype=jnp.int32)[:, None]
    ang = ((k1[None, :, None] * n[:, None, :]) % s).astype(F32) * (2.0 * math.pi / s)
    c, sn = jnp.cos(ang), jnp.sin(ang)
    tile = lambda m: m.reshape(RADIX, a // tm, tm, 2 * a)
    re = tile(jnp.concatenate([c, -sn], axis=-1))
    im = tile(jnp.concatenate([-sn, -c], axis=-1))
    return jnp.concatenate([re, im], axis=2).astype(BF16)


def _dil_kernel(q_ref, kp_ref, km_ref, kn_ref, vp_ref, vm_ref, vn_ref, o_ref, l_ref, *,
                dil, half, rows, n_p, sub_p, sub_s, slopes):
    i = pl.program_id(1)
    start = i * rows
    in_p = start < n_p
    seq_len = jnp.where(in_p, sub_p, sub_s)
    local = jnp.where(in_p, start % sub_p, (start - n_p) % sub_s)
    qi = lax.broadcasted_iota(jnp.int32, (HEAD, 2 * HEAD), 0)
    ci = lax.broadcasted_iota(jnp.int32, (HEAD, 2 * HEAD), 1)
    rel = ci - HALO - qi
    band = jnp.abs(rel) <= half
    dist = (jnp.abs(rel) * dil).astype(F32)
    nsub = rows // HEAD

    def window(main_ref, prev_ref, next_ref, r, s, sl):
        lo, hi = s * HEAD - HALO, (s + 1) * HEAD + HALO
        parts = [prev_ref[r, :, sl]] if lo < 0 else []
        parts.append(main_ref[r, max(lo, 0):min(hi, rows), sl])
        if hi > rows:
            parts.append(next_ref[r, :, sl])
        return parts[0] if len(parts) == 1 else jnp.concatenate(parts, axis=0)

    for s in range(nsub):
        kpos = local + s * HEAD - HALO + ci
        valid = band & (kpos >= 0) & (kpos < seq_len)
        qs = slice(s * HEAD, (s + 1) * HEAD)
        for h in range(C_PER):
            sl = slice(h * HEAD, (h + 1) * HEAD)
            bias = jnp.where(valid, -slopes[h] * dist, NEG)
            for r in range(q_ref.shape[0]):
                sc = _dot_nt(q_ref[r, qs, sl], window(km_ref, kp_ref, kn_ref, r, s, sl))
                sc = sc * SCALE + bias
                m = jnp.max(sc, axis=-1, keepdims=True)
                e = jnp.exp(sc - m)
                den = jnp.sum(e, axis=-1, keepdims=True)
                p = (e * (1.0 / den)).astype(BF16)
                o_ref[r, qs, sl] = _dot(p, window(vm_ref, vp_ref, vn_ref, r, s, sl))
                l_ref[r, qs, sl] = jnp.broadcast_to(m + jnp.log(den), (HEAD, HEAD))


def dil_attn(qkv, col0, p_idx, groups, slopes, rows_per_step=512):
    window, dil = C_PATTERNS[p_idx]
    half = window // (2 * dil)
    assert half <= HALO
    (_, bp, sp), (_, bs, ss) = groups
    n = qkv.shape[1]
    n_p, sub_p, sub_s = bp * sp // dil, sp // dil, ss // dil
    assert sub_p % HEAD == 0 and sub_s % HEAD == 0
    rows = _pick(math.gcd(sub_p, sub_s), rows_per_step)
    per = rows // HALO
    nhalo = n // HALO
    nres = math.gcd(dil, max(1, rows_per_step // rows))

    main = lambda c: pl.BlockSpec((nres, rows, C_OUT_W), lambda r, i: (r, i, col0 + c))
    prev = lambda c: pl.BlockSpec((nres, HALO, C_OUT_W),
                                  lambda r, i: (r, jnp.maximum(i * per - 1, 0), col0 + c))
    nxt = lambda c: pl.BlockSpec((nres, HALO, C_OUT_W),
                                 lambda r, i: (r, jnp.minimum((i + 1) * per, nhalo - 1), col0 + c))
    out_spec = pl.BlockSpec((nres, rows, C_OUT_W), lambda r, i: (r, i, 0))
    out_sds = jax.ShapeDtypeStruct((dil, n, C_OUT_W), F32)
    return pl.pallas_call(
        functools.partial(_dil_kernel, dil=dil, half=half, rows=rows, n_p=n_p, sub_p=sub_p, sub_s=sub_s,
                          slopes=tuple(slopes[p_idx * C_PER:(p_idx + 1) * C_PER])),
        grid=(dil // nres, n // rows),
        in_specs=[main(0), prev(1), main(1), nxt(1), prev(2), main(2), nxt(2)],
        out_specs=[out_spec, out_spec],
        out_shape=[out_sds, out_sds],
        compiler_params=_params(2),
        name=f"dil_attn_p{p_idx}",
    )(*([qkv] * 7))


def _dil_combine_kernel(*refs):
    srcs, out_ref, scr = refs[:2 * N_PAT], refs[2 * N_PAT], list(refs[2 * N_PAT + 1:])
    vals = []
    for src in srcs:
        dil = src.shape[0]
        if dil == 1:
            vals.append(src[0])
            continue
        nat = scr.pop(0)
        nchunk = nat.shape[0]
        for r in range(dil):
            for c in range(nchunk):
                nat[c, pl.ds(r, src.shape[1], stride=dil), :] = src[r, :, c * LANE:(c + 1) * LANE]
        vals.append(jnp.concatenate([nat[c] for c in range(nchunk)], axis=-1))
    outs, lses = vals[:N_PAT], vals[N_PAT:]
    m = functools.reduce(jnp.maximum, lses)
    es = [jnp.exp(l - m) for l in lses]
    num = sum(e * o for e, o in zip(es, outs))
    out_ref[...] = (num / sum(es)).astype(out_ref.dtype)


def dil_combine(outs, lses, tr=512):
    t = outs[0].shape[0] * outs[0].shape[1]
    tr = _pick(t, tr, 8 * max(d for _, d in C_PATTERNS))
    spec = lambda a: pl.BlockSpec((a.shape[0], tr // a.shape[0], C_OUT_W), lambda i: (0, i, 0))
    n_scr = sum(2 for a in outs if a.shape[0] > 1)
    return pl.pallas_call(
        _dil_combine_kernel,
        grid=(t // tr,),
        in_specs=[spec(a) for a in outs + lses],
        out_specs=pl.BlockSpec((tr, C_OUT_W), lambda i: (i, 0)),
        out_shape=jax.ShapeDtypeStruct((t, C_OUT_W), BF16),
        scratch_shapes=[pltpu.VMEM((C_OUT_W // LANE, tr, LANE), F32)] * n_scr,
        compiler_params=_params(1),
        name="dil_combine",
    )(*outs, *lses)


def _xattn_block_kernel(o_ref, x_ref, gmix_ref, gpre_ref, wq_ref, kv_ref, wo_ref, gpost_ref, gnext_ref,
                        xo_ref, h_ref):
    xn = x_ref[...] + _rms(o_ref[...].astype(F32), gmix_ref[...])
    q = _dot(_rms(xn, gpre_ref[...]).astype(BF16), wq_ref[...]).astype(BF16)
    heads = []
    for h in range(X_HEADS):
        sl = slice(h * HEAD, (h + 1) * HEAD)
        s = _dot_nt(q[:, sl], kv_ref[:, sl]) * SCALE
        p = jnp.exp(s - jnp.max(s, axis=-1, keepdims=True))
        l = jnp.sum(p, axis=-1, keepdims=True)
        o = _dot(p.astype(BF16), kv_ref[:, X_W + h * HEAD:X_W + (h + 1) * HEAD]) * (1.0 / l)
        heads.append(o.astype(BF16))
    xn = xn + _rms(_dot(jnp.concatenate(heads, axis=-1), wo_ref[...]), gpost_ref[...])
    xo_ref[...] = xn
    h_ref[...] = _rms(xn, gnext_ref[...]).astype(h_ref.dtype)


def xattn_block(o, x, g_mix_post, g_pre, w_xq, kv, w_xo, g_post, g_next, layer, groups, n_mem, tm=256):
    t, d = x.shape
    (_, bp, sp), (_, _, ss) = groups
    tm = _pick(math.gcd(sp, ss), tm, 8)
    n_p = bp * sp // tm

    def seq_of(i):
        return jnp.where(i < n_p, i // (sp // tm), bp + (i - n_p) // (ss // tm))

    row = pl.BlockSpec((tm, d), lambda i: (i, 0))
    vec = _wspec(layer, (1, d), lambda i: (0, 0))
    g3 = lambda g: g.reshape(g.shape[0], 1, d)
    return pl.pallas_call(
        _xattn_block_kernel,
        grid=(t // tm,),
        in_specs=[row, row, vec, vec,
                  _wspec(layer, (d, X_W), lambda i: (0, 0)),
                  pl.BlockSpec((n_mem, 2 * X_W), lambda i: (seq_of(i), 0)),
                  _wspec(layer, (X_W, d), lambda i: (0, 0)), vec, vec],
        out_specs=[row, row],
        out_shape=[jax.ShapeDtypeStruct((t, d), F32), jax.ShapeDtypeStruct((t, d), BF16)],
        compiler_params=_params(1),
        name="xattn_block",
    )(o, x, g3(g_mix_post), g3(g_pre), w_xq, kv, w_xo, g3(g_post), g3(g_next))


def _rope_tables(seq_len):
    t = jnp.arange(seq_len)
    row = (t // GRID_W).astype(F32)
    col = (t % GRID_W).astype(F32)
    n_pairs = HEAD // 4
    inv_freq = ROPE_BASE ** (-jnp.arange(n_pairs, dtype=F32) / n_pairs)
    ang = jnp.concatenate([row[:, None] * inv_freq[None, :], col[:, None] * inv_freq[None, :]], axis=-1)
    cos = jnp.repeat(jnp.cos(ang), 2, axis=-1)
    sin = jnp.repeat(jnp.sin(ang), 2, axis=-1)
    even = (jnp.arange(HEAD) % 2 == 0)[None, :]
    return cos, jnp.where(even, -sin, 0.0), jnp.where(even, 0.0, sin)


def _alibi_slopes():
    h = np.arange(1, C_HEADS + 1, dtype=np.float32)
    return [float(v) for v in np.float32(2.0) ** (-np.float32(ALIBI_MAX_EXP) * h / np.float32(C_HEADS))]


def kernel(x_prompt, x_sample, mem_prompt, mem_sample, ffn1_pre_norm, ffn1_w_in, ffn1_w_out, ffn1_post_norm, mix_pre_norm, w_mix_in, a_q_norm, a_k_norm, w_a_proj, w_b_proj, w_c_proj, w_branch_gate, b_branch_gate, w_mix_out, mix_post_norm, xattn_pre_norm, mem_norm, w_xq, w_xkv, w_xo, xattn_post_norm, ffn2_pre_norm, ffn2_w_in, ffn2_w_out, ffn2_post_norm):
    bp, sp, d = x_prompt.shape
    bs, ss, _ = x_sample.shape
    n_mem = mem_prompt.shape[1]
    depth = ffn1_w_in.shape[0]
    tp = bp * sp
    groups = ((0, bp, sp), (tp, bs, ss))
    assert tp % ss == 0 and sp % GRID_W == 0 and ss % GRID_W == 0

    x = (x_prompt.reshape(tp, d), x_sample.reshape(bs * ss, d))
    mem = jnp.concatenate([mem_prompt.reshape(bp * n_mem, d), mem_sample.reshape(bs * n_mem, d)], axis=0)

    rope = _rope_tables(max(sp, ss))
    dft = {s: _dft_tables(s) for s in {sp, ss}}
    kc = np.arange(HEAD)
    ang128 = 2.0 * np.pi * ((kc[:, None] * kc[None, :]) % HEAD) / HEAD
    cs128 = jnp.asarray(np.concatenate([np.cos(ang128), np.sin(ang128)], axis=1), BF16)
    slopes = _alibi_slopes()
    head_scale = jnp.concatenate([jnp.full((A_Q_HEADS, HEAD), SCALE * math.log2(math.e), F32),
                                  jnp.ones((A_KV_HEADS, HEAD), F32)], axis=0)
    qk_gains = jnp.concatenate([jnp.repeat(a_q_norm[:, None, :], A_Q_HEADS, axis=1),
                                jnp.repeat(a_k_norm[:, None, :], A_KV_HEADS, axis=1)], axis=1)

    bf = lambda w: w.astype(BF16)
    w_mix_out = bf(w_mix_out)
    w_a_proj, w_b_proj, w_c_proj = bf(w_a_proj), bf(w_b_proj), bf(w_c_proj)
    w_xq, w_xkv, w_xo = bf(w_xq), bf(w_xkv), bf(w_xo)

    FFN_OUT_TN, MERGE_TN = _pick(d, 512), _pick(d, 256)
    nat_block = lambda j: jnp.where(j < NAT_V, BLK_B + j,
                                    jnp.where(j == NAT_V, BLK_V, BLK_CQ + N_PAT * (j - NAT_C)))

    h = rms_rows(x, ffn1_pre_norm, 0)
    for l in range(depth):
        act, w_out = swiglu_in(h, ffn1_w_in, l, riders=[(ffn1_w_out, FFN_OUT_TN)])
        o, w_gate, w_min = matmul(act, w_out, None, BF16, tm=512, name="ffn_out",
                                  riders=[(w_branch_gate, MERGE_TN), (w_mix_in, None)], rider_layer=l)
        w_min = w_min[None]
        x, u = resid_norm(o, x, ffn1_post_norm, l, mix_pre_norm, l, 0.5, tp)

        qk = mix_qk(u, w_min, 0, qk_gains, l, head_scale, rope, groups)
        nat = matmul(u, w_min, 0, BF16, tn=C_OUT_W, n_out=NAT_W, col_block=nat_block, name="mix_nat")
        y_a = y_b = None
        for g in groups:
            y_a = attn_a(qk, nat, g, y_a)
        zc, zs = chan_dft(nat, cs128)
        for g in groups:
            y_b = seq_dft(zc, zs, dft[g[2]], g, y_b)
        outs, lses = [], []
        for p_idx, (_, dil) in enumerate(C_PATTERNS):
            if dil == 1:
                qkv, col0 = nat.reshape(1, *nat.shape), NAT_C
            else:
                qkv, col0 = mix_dil(u, w_min, 0, p_idx), 0
            o_p, l_p = dil_attn(qkv, col0, p_idx, groups, slopes)
            outs.append(o_p)
            lses.append(l_p)
        y_c = dil_combine(outs, lses)
        merged = merge_branches(u, y_a, y_b, y_c, w_gate, b_branch_gate,
                                w_a_proj, w_b_proj, w_c_proj, l)
        o = matmul(merged, w_mix_out, l, BF16, name="mix_out")
        kv = matmul(rms_rows(mem, mem_norm, l), w_xkv, l, BF16, name="xattn_kv")
        x, h = xattn_block(o, x, mix_post_norm, xattn_pre_norm, w_xq, kv, w_xo, xattn_post_norm,
                           ffn2_pre_norm, l, groups, n_mem)

        act, w_out = swiglu_in(h, ffn2_w_in, l, riders=[(ffn2_w_out, FFN_OUT_TN)])
        o = matmul(act, w_out, None, BF16, tm=512, name="ffn_out")
        if l + 1 < depth:
            x, h = resid_norm(o, x, ffn2_post_norm, l, ffn1_pre_norm, l + 1, 0.5, tp)
        else:
            y_p, y_s = resid_norm(o, x, ffn2_post_norm, l, None, None, 0.5, tp)

    return y_p.reshape(bp, sp, d), y_s.reshape(bs, ss, d)
```

```python
import functools
import math

import numpy as np
import jax
import jax.numpy as jnp
from jax import lax
from jax.experimental import pallas as pl
from jax.experimental.pallas import tpu as pltpu

F32 = jnp.float32
BF16 = jnp.bfloat16

HEAD = 128
GRID_W = 64
EPS = 1e-6
ROPE_BASE = 10000.0
A_Q_HEADS = 12
A_KV_HEADS = 4
A_GROUPS = A_Q_HEADS // A_KV_HEADS
B_GROUPS = 20
C_PATTERNS = ((128, 1), (512, 4), (2048, 16))
C_PER = 4
C_HEADS = C_PER * len(C_PATTERNS)
ALIBI_MAX_EXP = 8.0
X_HEADS = 4
N_BRANCH = 3

A_Q_W = A_Q_HEADS * HEAD
A_KV_W = A_KV_HEADS * HEAD
QK_W = A_Q_W + A_KV_W
B_W = B_GROUPS * HEAD
C_W = C_HEADS * HEAD
C_OUT_W = C_PER * HEAD
X_W = X_HEADS * HEAD
MIX_W = A_Q_W + 2 * A_KV_W + B_W + 3 * C_W
BLK_V = QK_W // C_OUT_W
BLK_CQ = (QK_W + A_KV_W + B_W) // C_OUT_W
N_PAT = len(C_PATTERNS)
BLK_B = (QK_W + A_KV_W) // C_OUT_W
NAT_W = B_W + A_KV_W + 3 * C_OUT_W
NAT_V = B_W // C_OUT_W
NAT_C = NAT_V + 1
SCALE = HEAD ** -0.5
NEG = -1e30
HALO = HEAD // 2

VMEM_LIMIT_V7X = 56 * 1024 * 1024
LANE = 128


def _params(n_axes):
    return pltpu.CompilerParams(dimension_semantics=("arbitrary",) * n_axes,
                                vmem_limit_bytes=VMEM_LIMIT_V7X)


def _pick(n, pref, unit=LANE):
    if n <= pref:
        return n
    best = None
    for t in range(unit, pref + 1, unit):
        if n % t == 0:
            best = t
    assert best is not None, (n, pref)
    return best


def _wspec(layer, shape, imap):
    return pl.BlockSpec((None,) + shape, lambda *g: (layer,) + tuple(imap(*g)))


def _rms(x, g):
    return x * lax.rsqrt(jnp.mean(x * x, axis=-1, keepdims=True) + EPS) * g


def _sigmoid(x):
    return 1.0 / (1.0 + jnp.exp(-x))


def _dot(a, b):
    return jnp.dot(a, b, preferred_element_type=F32)


def _dot_nt(a, b):
    return lax.dot_general(a, b, (((1,), (1,)), ((), ())), preferred_element_type=F32)


def _row_inputs(x, tr):
    if not isinstance(x, tuple):
        return [x], [pl.BlockSpec((tr, x.shape[1]), lambda i: (i, 0))], None
    xp, xs = x
    n_p = xp.shape[0] // tr
    d = xp.shape[1]
    return ([xp, xs],
            [pl.BlockSpec((tr, d), lambda i: (jnp.minimum(i, n_p - 1), 0)),
             pl.BlockSpec((tr, d), lambda i: (jnp.maximum(i - n_p, 0), 0))], n_p)


def _load_rows(x_refs, n_p):
    if len(x_refs) == 1:
        return x_refs[0][...]
    return jnp.where(pl.program_id(0) < n_p, x_refs[0][...], x_refs[1][...])


def _rms_kernel(*refs, n_p):
    x_refs, g_ref, o_ref = refs[:-2], refs[-2], refs[-1]
    o_ref[...] = _rms(_load_rows(x_refs, n_p), g_ref[...]).astype(o_ref.dtype)


def rms_rows(x, g, layer, tr=256):
    pieces = x if isinstance(x, tuple) else (x,)
    t = sum(a.shape[0] for a in pieces)
    d = g.shape[-1]
    tr = _pick(math.gcd(*[a.shape[0] for a in pieces]), tr, 8)
    arrs, specs, n_p = _row_inputs(x, tr)
    return pl.pallas_call(
        functools.partial(_rms_kernel, n_p=n_p),
        grid=(t // tr,),
        in_specs=specs + [_wspec(layer, (1, d), lambda i: (0, 0))],
        out_specs=pl.BlockSpec((tr, d), lambda i: (i, 0)),
        out_shape=jax.ShapeDtypeStruct((t, d), BF16),
        compiler_params=_params(1),
        name="rms_rows",
    )(*arrs, g.reshape(g.shape[0], 1, d))


def _resid_kernel(*refs, coef, n_x, n_p, has_next):
    o_ref, x_refs, gp_ref, rest = refs[0], refs[1:1 + n_x], refs[1 + n_x], refs[2 + n_x:]
    xn = _load_rows(x_refs, n_p) + coef * _rms(o_ref[...].astype(F32), gp_ref[...])
    if has_next:
        gn_ref, xo_ref, h_ref = rest
        xo_ref[...] = xn
        h_ref[...] = _rms(xn, gn_ref[...]).astype(h_ref.dtype)
    else:
        yp_ref, ys_ref = rest
        i = pl.program_id(0)

        @pl.when(i < n_p)
        def _():
            yp_ref[...] = xn

        @pl.when(i >= n_p)
        def _():
            ys_ref[...] = xn


def resid_norm(o, x, g_post, layer, g_next, next_layer, coef, rows_p, tr=256):
    t, d = o.shape
    tr = _pick(math.gcd(rows_p, t - rows_p), tr, 8)
    arrs, specs, n_p = _row_inputs(x, tr)
    row = pl.BlockSpec((tr, d), lambda i: (i, 0))
    vec = lambda lyr: _wspec(lyr, (1, d), lambda i: (0, 0))
    g3 = lambda g: g.reshape(g.shape[0], 1, d)
    if g_next is None:
        n_p = rows_p // tr
        return pl.pallas_call(
            functools.partial(_resid_kernel, coef=coef, n_x=len(arrs), n_p=n_p, has_next=False),
            grid=(t // tr,),
            in_specs=[row] + specs + [vec(layer)],
            out_specs=[pl.BlockSpec((tr, d), lambda i: (jnp.minimum(i, n_p - 1), 0)),
                       pl.BlockSpec((tr, d), lambda i: (jnp.maximum(i - n_p, 0), 0))],
            out_shape=[jax.ShapeDtypeStruct((rows_p, d), F32), jax.ShapeDtypeStruct((t - rows_p, d), F32)],
            compiler_params=_params(1),
            name="resid_last",
        )(o, *arrs, g3(g_post))
    return pl.pallas_call(
        functools.partial(_resid_kernel, coef=coef, n_x=len(arrs), n_p=n_p, has_next=True),
        grid=(t // tr,),
        in_specs=[row] + specs + [vec(layer), vec(next_layer)],
        out_specs=[row, row],
        out_shape=[jax.ShapeDtypeStruct((t, d), F32), jax.ShapeDtypeStruct((t, d), BF16)],
        compiler_params=_params(1),
        name="resid_norm",
    )(o, *arrs, g3(g_post), g3(g_next))


BF16_ROWS = 16


def _with_riders(body, n_in, n_out, n_riders):
    def kern(*refs):
        ins, r_in = refs[:n_in], refs[n_in:n_in + n_riders]
        outs = refs[n_in + n_riders:n_in + n_riders + n_out]
        r_out = refs[n_in + n_riders + n_out:n_in + 2 * n_riders + n_out]
        body(*ins, *outs, *refs[n_in + 2 * n_riders + n_out:])
        for src, dst in zip(r_in, r_out):
            dst[...] = src[...].astype(dst.dtype)
    return kern


def _rider_specs(riders, layer, grid):
    n_steps = grid[0] * grid[1]
    in_specs, out_specs, out_shapes = [], [], []
    for w in riders:
        _, k, n = w.shape
        assert k % (n_steps * BF16_ROWS) == 0, (w.shape, grid)
        slab = (k // n_steps, n)
        step = lambda i, j: (i * grid[1] + j, 0)
        in_specs.append(_wspec(layer, slab, step))
        out_specs.append(pl.BlockSpec(slab, step))
        out_shapes.append(jax.ShapeDtypeStruct((k, n), BF16))
    return in_specs, out_specs, out_shapes


def _mm_kernel(x_ref, w_ref, o_ref):
    o_ref[...] = _dot(x_ref[...], w_ref[...]).astype(o_ref.dtype)


def matmul(x, w, layer, out_dtype, tm=1024, tn=1024, n_out=None, col_block=None, name="matmul",
           riders=(), rider_layer=None):
    m, k = x.shape
    n = w.shape[2] if n_out is None else n_out
    tm = _pick(m, tm, 8)
    tn = _pick(n, tn)
    col_block = col_block or (lambda j: j)
    grid = (m // tm, n // tn)
    r_in, r_out, r_shapes = _rider_specs(riders, rider_layer, grid)
    out = pl.pallas_call(
        _with_riders(_mm_kernel, 2, 1, len(riders)),
        grid=grid,
        in_specs=[pl.BlockSpec((tm, k), lambda i, j: (i, 0)),
                  _wspec(layer, (k, tn), lambda i, j: (0, col_block(j)))] + r_in,
        out_specs=[pl.BlockSpec((tm, tn), lambda i, j: (i, j))] + r_out,
        out_shape=[jax.ShapeDtypeStruct((m, n), out_dtype)] + r_shapes,
        compiler_params=_params(2),
        name=name,
    )(x, w, *riders)
    return out if riders else out[0]


SWIGLU_PARTS = 2


def _swiglu_kernel(x_ref, wg_ref, wu_ref, o_ref):
    wg = wg_ref[...].astype(x_ref.dtype)
    wu = wu_ref[...].astype(x_ref.dtype)
    part = x_ref.shape[0] // SWIGLU_PARTS
    for p in range(SWIGLU_PARTS):
        rows = slice(p * part, (p + 1) * part)
        x = x_ref[rows, :]
        g = _dot(x, wg)
        u = _dot(x, wu)
        o_ref[rows, :] = (g * _sigmoid(g) * u).astype(o_ref.dtype)


def swiglu_in(h, w_in, layer, tm=2048, tn=256, riders=()):
    t, d = h.shape
    f = w_in.shape[2] // 2
    tm = _pick(t, tm, 8)
    tn = _pick(f, tn)
    nj = f // tn
    grid = (t // tm, nj)
    r_in, r_out, r_shapes = _rider_specs(riders, layer, grid)
    return pl.pallas_call(
        _with_riders(_swiglu_kernel, 3, 1, len(riders)),
        grid=grid,
        in_specs=[pl.BlockSpec((tm, d), lambda i, j: (i, 0)),
                  _wspec(layer, (d, tn), lambda i, j: (0, j)),
                  _wspec(layer, (d, tn), lambda i, j: (0, nj + j))] + r_in,
        out_specs=[pl.BlockSpec((tm, tn), lambda i, j: (i, j))] + r_out,
        out_shape=[jax.ShapeDtypeStruct((t, f), BF16)] + r_shapes,
        compiler_params=_params(2),
        name="swiglu_in",
    )(h, w_in, w_in, *riders)


def _merge_kernel(u_ref, a_ref, b_ref, c_ref, wg0_ref, wg1_ref, wg2_ref, bg0_ref, bg1_ref, bg2_ref,
                  wa_ref, wb_ref, wc_ref, o_ref):
    u = u_ref[...]
    acc = _sigmoid(_dot(u, wg0_ref[...]) + bg0_ref[...]) * _dot(a_ref[...], wa_ref[...])
    acc += _sigmoid(_dot(u, wg1_ref[...]) + bg1_ref[...]) * _dot(b_ref[...], wb_ref[...])
    acc += _sigmoid(_dot(u, wg2_ref[...]) + bg2_ref[...]) * _dot(c_ref[...], wc_ref[...])
    o_ref[...] = acc.astype(o_ref.dtype)


def merge_branches(u, a, b, c, w_gate, gate_layer, b_gate, w_a, w_b, w_c, layer, tm=512, tn=256):
    t, d = u.shape
    tm = _pick(t, tm, 8)
    tn = _pick(d, tn)
    nj = d // tn
    act = lambda width: pl.BlockSpec((tm, width), lambda i, j: (i, 0))
    gate_w = lambda br: _wspec(gate_layer, (d, tn), lambda i, j: (0, br * nj + j))
    gate_b = lambda br: _wspec(layer, (1, tn), lambda i, j: (0, br * nj + j))
    proj_w = lambda width: _wspec(layer, (width, tn), lambda i, j: (0, j))
    bg = b_gate.reshape(b_gate.shape[0], 1, N_BRANCH * d)
    return pl.pallas_call(
        _merge_kernel,
        grid=(t // tm, nj),
        in_specs=[act(d), act(A_Q_W), act(B_W), act(C_OUT_W),
                  gate_w(0), gate_w(1), gate_w(2), gate_b(0), gate_b(1), gate_b(2),
                  proj_w(A_Q_W), proj_w(B_W), proj_w(C_OUT_W)],
        out_specs=pl.BlockSpec((tm, tn), lambda i, j: (i, j)),
        out_shape=jax.ShapeDtypeStruct((t, d), BF16),
        compiler_params=_params(2),
        name="merge_branches",
    )(u, a, b, c, w_gate, w_gate, w_gate, bg, bg, bg, w_a, w_b, w_c)


QK_PARTS = 4


def _mix_qk_kernel(u_ref, w_ref, g_ref, sc_ref, cs_ref, sa_ref, sb_ref, o_ref):
    w = w_ref[...]
    part = u_ref.shape[0] // QK_PARTS
    for p in range(QK_PARTS):
        rows = slice(p * part, (p + 1) * part)
        y = _dot(u_ref[rows, :], w)
        cs, sa, sb = cs_ref[rows, :], sa_ref[rows, :], sb_ref[rows, :]
        for h in range(o_ref.shape[1] // HEAD):
            sl = slice(h * HEAD, (h + 1) * HEAD)
            x = _rms(y[:, sl], g_ref[h:h + 1, :])
            r = x * cs + pltpu.roll(x, HEAD - 1, 1) * sa + pltpu.roll(x, 1, 1) * sb
            o_ref[rows, sl] = (r * sc_ref[h:h + 1, :]).astype(o_ref.dtype)


def mix_qk(u, w_mix_in, layer, gains, gain_layer, scales, rope, groups, tm=1024, tn=1024):
    t, d = u.shape
    (_, bp, sp), (_, _, ss) = groups
    tm = _pick(math.gcd(sp, ss), tm, 8)
    tn = _pick(QK_W, tn)
    n_p = bp * sp // tm
    hpt = tn // HEAD

    def pos_block(i):
        return jnp.where(i < n_p, i % (sp // tm), (i - n_p) % (ss // tm))

    tab = pl.BlockSpec((tm, HEAD), lambda i, j: (pos_block(i), 0))
    return pl.pallas_call(
        _mix_qk_kernel,
        grid=(t // tm, QK_W // tn),
        in_specs=[pl.BlockSpec((tm, d), lambda i, j: (i, 0)),
                  _wspec(layer, (d, tn), lambda i, j: (0, j)),
                  _wspec(gain_layer, (hpt, HEAD), lambda i, j: (j, 0)),
                  pl.BlockSpec((hpt, HEAD), lambda i, j: (j, 0)),
                  tab, tab, tab],
        out_specs=pl.BlockSpec((tm, tn), lambda i, j: (i, j)),
        out_shape=jax.ShapeDtypeStruct((t, QK_W), BF16),
        compiler_params=_params(2),
        name="mix_qk",
    )(u, w_mix_in, gains, scales, *rope)


def _mix_dil_kernel(u_ref, w_ref, o_ref, scr_ref, *, dil):
    w = w_ref[...]
    nchunk, tm, _ = scr_ref.shape
    part = max(tm // QK_PARTS, BF16_ROWS * dil)
    for p in range(tm // part):
        y = _dot(u_ref[p * part:(p + 1) * part, :], w)
        sub = slice(p * part // dil, (p + 1) * part // dil)
        for c in range(nchunk):
            scr_ref[c, p * part:(p + 1) * part, :] = y[:, c * LANE:(c + 1) * LANE]
            for r in range(dil):
                o_ref[r, sub, c * LANE:(c + 1) * LANE] = (
                    scr_ref[c, pl.ds(p * part + r, part // dil, stride=dil), :].astype(o_ref.dtype))


def mix_dil(u, w_mix_in, layer, p_idx, tm=1024):
    t, d = u.shape
    dil = C_PATTERNS[p_idx][1]
    tm = _pick(t, tm, 8 * dil)
    return pl.pallas_call(
        functools.partial(_mix_dil_kernel, dil=dil),
        grid=(t // tm, 3),
        in_specs=[pl.BlockSpec((tm, d), lambda i, j: (i, 0)),
                  _wspec(layer, (d, C_OUT_W), lambda i, j: (0, BLK_CQ + p_idx + N_PAT * j))],
        out_specs=pl.BlockSpec((dil, tm // dil, C_OUT_W), lambda i, j: (0, i, j)),
        out_shape=jax.ShapeDtypeStruct((dil, t // dil, 3 * C_OUT_W), BF16),
        scratch_shapes=[pltpu.VMEM((C_OUT_W // LANE, tm, LANE), F32)],
        compiler_params=_params(2),
        name=f"mix_dil_p{p_idx}",
    )(u, w_mix_in)


def _attn_a_kernel(q_ref, k_ref, v_ref, *rest):
    o_ref = rest[-1]
    k = k_ref[...]
    v = v_ref[...]
    part = q_ref.shape[0] // ATTN_PARTS
    for r in range(ATTN_PARTS):
        rows = slice(r * part, (r + 1) * part)
        for g in range(A_GROUPS):
            sl = slice(g * HEAD, (g + 1) * HEAD)
            s = _dot_nt(q_ref[rows, sl], k)
            p = jnp.exp2(s - jnp.max(s, axis=-1, keepdims=True))
            l = jnp.sum(p, axis=-1, keepdims=True)
            o = _dot(p.astype(BF16), v) * (1.0 / l)
            o_ref[rows, sl] = o.astype(o_ref.dtype)


ATTN_PARTS = 4


def attn_a(qk, nat, group, prev_out, tq=1024):
    row0, nseq, s = group
    t = qk.shape[0]
    tq = _pick(s, tq, 8)
    nq = s // tq
    rb0, sb0 = row0 // tq, row0 // s
    gw = A_GROUPS * HEAD
    in_specs = [pl.BlockSpec((tq, gw), lambda b, h, i: (rb0 + b * nq + i, h)),
                pl.BlockSpec((s, HEAD), lambda b, h, i: (sb0 + b, A_Q_HEADS + h)),
                pl.BlockSpec((s, HEAD), lambda b, h, i: (sb0 + b, NAT_V * (C_OUT_W // HEAD) + h))]
    args = [qk, qk, nat]
    aliases = {}
    if prev_out is not None:
        in_specs.append(pl.BlockSpec(memory_space=pl.ANY))
        args.append(prev_out)
        aliases = {3: 0}
    return pl.pallas_call(
        _attn_a_kernel,
        grid=(nseq, A_KV_HEADS, nq),
        in_specs=in_specs,
        out_specs=pl.BlockSpec((tq, gw), lambda b, h, i: (rb0 + b * nq + i, h)),
        out_shape=jax.ShapeDtypeStruct((t, A_Q_W), BF16),
        input_output_aliases=aliases,
        compiler_params=_params(3),
        name="attn_a",
    )(*args)


RADIX = 8
DFT_TM = 256


def _chan_dft_kernel(z_ref, cs_ref, zc_ref, zs_ref, scr_ref):
    cs = cs_ref[...]
    rows = zc_ref.shape[1]
    for g in range(B_GROUPS):
        sl = slice(g * HEAD, (g + 1) * HEAD)
        y = _dot(z_ref[:, sl], cs) * (HEAD ** -0.5)
        scr_ref[2 * g] = y[:, :HEAD]
        scr_ref[2 * g + 1] = y[:, HEAD:]
        for b in range(RADIX):
            zc_ref[b, :, sl] = scr_ref[2 * g, pl.ds(b, rows, stride=RADIX), :].astype(zc_ref.dtype)
            zs_ref[b, :, sl] = scr_ref[2 * g + 1, pl.ds(b, rows, stride=RADIX), :].astype(zs_ref.dtype)


def chan_dft(nat, cs128, tr=512):
    t = nat.shape[0]
    tr = _pick(t, tr, 16 * RADIX)
    out = pl.BlockSpec((RADIX, tr // RADIX, B_W), lambda i: (0, i, 0))
    return pl.pallas_call(
        _chan_dft_kernel,
        grid=(t // tr,),
        in_specs=[pl.BlockSpec((tr, B_W), lambda i: (i, 0)),
                  pl.BlockSpec((HEAD, 2 * HEAD), lambda i: (0, 0))],
        out_specs=[out, out],
        out_shape=[jax.ShapeDtypeStruct((RADIX, t // RADIX, B_W), BF16)] * 2,
        scratch_shapes=[pltpu.VMEM((2 * B_GROUPS, tr, LANE), F32)],
        compiler_params=_params(1),
        name="chan_dft",
    )(nat, cs128)


def _outer_dft_real(tre, tim, k2):
    sums = {}
    for b in range(RADIX):
        ang = 2.0 * math.pi * ((k2 * b) % RADIX) / RADIX
        for coef, tile in ((math.cos(ang), tre[b]), (math.sin(ang), tim[b])):
            mag = round(abs(coef), 9)
            if mag == 0.0:
                continue
            term = tile if coef > 0 else -tile
            sums[mag] = term if mag not in sums else sums[mag] + term
    return sum(v if mag == 1.0 else v * mag for mag, v in sums.items())


def _seq_dft_kernel(m_ref, zc_ref, zs_ref, *rest, scale):
    o_ref = rest[-1]
    a = zc_ref.shape[1]
    tm = m_ref.shape[1] // 2
    tre, tim = [], []
    for b in range(RADIX):
        t = _dot(m_ref[b, :, :a], zc_ref[b]) + _dot(m_ref[b, :, a:], zs_ref[b])
        tre.append(t[:tm])
        tim.append(t[tm:])
    for k2 in range(RADIX):
        o_ref[k2] = (_outer_dft_real(tre, tim, k2) * scale).astype(o_ref.dtype)


def seq_dft(zc, zs, tables, group, prev_out, tn=256):
    row0, nseq, s = group
    t = zc.shape[1] * RADIX
    a = s // RADIX
    ni, tm2 = tables.shape[1], tables.shape[2]
    tm = tm2 // 2
    tn = _pick(B_W, tn)
    nj = B_W // tn
    sb0 = row0 // s
    zspec = pl.BlockSpec((RADIX, a, tn), lambda i, b, j: (0, sb0 + b, j))
    in_specs = [pl.BlockSpec((RADIX, None, tm2, 2 * a), lambda i, b, j: (0, i, 0, 0)), zspec, zspec]
    args = [tables, zc, zs]
    aliases = {}
    out_shape = (t // a, ni, tm, B_W)
    if prev_out is not None:
        in_specs.append(pl.BlockSpec(memory_space=pl.ANY))
        args.append(prev_out.reshape(out_shape))
        aliases = {3: 0}
    return pl.pallas_call(
        functools.partial(_seq_dft_kernel, scale=s ** -0.5),
        grid=(ni, nseq, nj),
        in_specs=in_specs,
        out_specs=pl.BlockSpec((RADIX, None, tm, tn), lambda i, b, j: (sb0 + b, i, 0, j)),
        out_shape=jax.ShapeDtypeStruct(out_shape, BF16),
        input_output_aliases=aliases,
        compiler_params=_params(3),
        name="seq_dft",
    )(*args).reshape(t, B_W)


def _dft_tables(s):
    a = s // RADIX
    tm = min(a, DFT_TM)
    k1 = jnp.arange(a, dtype=jnp.int32)
    n = RADIX * jnp.arange(a, dtype=jnp.int32)[None, :] + jnp.arange(RADIX, dtype=jnp.int32)[:, None]
    ang = ((k1[None, :, None] * n[:, None, :]) % s).astype(F32) * (2.0 * math.pi / s)
    c, sn = jnp.cos(ang), jnp.sin(ang)
    tile = lambda m: m.reshape(RADIX, a // tm, tm, 2 * a)
    re = tile(jnp.concatenate([c, -sn], axis=-1))
    im = tile(jnp.concatenate([-sn, -c], axis=-1))
    return jnp.concatenate([re, im], axis=2).astype(BF16)


def _dil_kernel(q_ref, kp_ref, km_ref, kn_ref, vp_ref, vm_ref, vn_ref, o_ref, l_ref, *,
                dil, half, rows, n_p, sub_p, sub_s, slopes):
    i = pl.program_id(1)
    start = i * rows
    in_p = start < n_p
    seq_len = jnp.where(in_p, sub_p, sub_s)
    local = jnp.where(in_p, start % sub_p, (start - n_p) % sub_s)
    qi = lax.broadcasted_iota(jnp.int32, (HEAD, 2 * HEAD), 0)
    ci = lax.broadcasted_iota(jnp.int32, (HEAD, 2 * HEAD), 1)
    rel = ci - HALO - qi
    band = jnp.abs(rel) <= half
    dist = (jnp.abs(rel) * dil).astype(F32)
    nsub = rows // HEAD

    def window(main_ref, prev_ref, next_ref, r, s, sl):
        lo, hi = s * HEAD - HALO, (s + 1) * HEAD + HALO
        parts = [prev_ref[r, :, sl]] if lo < 0 else []
        parts.append(main_ref[r, max(lo, 0):min(hi, rows), sl])
        if hi > rows:
            parts.append(next_ref[r, :, sl])
        return parts[0] if len(parts) == 1 else jnp.concatenate(parts, axis=0)

    for s in range(nsub):
        kpos = local + s * HEAD - HALO + ci
        valid = band & (kpos >= 0) & (kpos < seq_len)
        qs = slice(s * HEAD, (s + 1) * HEAD)
        for h in range(C_PER):
            sl = slice(h * HEAD, (h + 1) * HEAD)
            bias = jnp.where(valid, -slopes[h] * dist, NEG)
            for r in range(q_ref.shape[0]):
                sc = _dot_nt(q_ref[r, qs, sl], window(km_ref, kp_ref, kn_ref, r, s, sl))
                sc = sc * SCALE + bias
                m = jnp.max(sc, axis=-1, keepdims=True)
                e = jnp.exp(sc - m)
                den = jnp.sum(e, axis=-1, keepdims=True)
                p = (e * (1.0 / den)).astype(BF16)
                o_ref[r, qs, sl] = _dot(p, window(vm_ref, vp_ref, vn_ref, r, s, sl))
                l_ref[r, qs, sl] = jnp.broadcast_to(m + jnp.log(den), (HEAD, HEAD))


def dil_attn(qkv, col0, p_idx, groups, slopes, rows_per_step=512):
    window, dil = C_PATTERNS[p_idx]
    half = window // (2 * dil)
    assert half <= HALO
    (_, bp, sp), (_, bs, ss) = groups
    n = qkv.shape[1]
    n_p, sub_p, sub_s = bp * sp // dil, sp // dil, ss // dil
    assert sub_p % HEAD == 0 and sub_s % HEAD == 0
    rows = _pick(math.gcd(sub_p, sub_s), rows_per_step)
    per = rows // HALO
    nhalo = n // HALO
    nres = math.gcd(dil, max(1, rows_per_step // rows))

    main = lambda c: pl.BlockSpec((nres, rows, C_OUT_W), lambda r, i: (r, i, col0 + c))
    prev = lambda c: pl.BlockSpec((nres, HALO, C_OUT_W),
                                  lambda r, i: (r, jnp.maximum(i * per - 1, 0), col0 + c))
    nxt = lambda c: pl.BlockSpec((nres, HALO, C_OUT_W),
                                 lambda r, i: (r, jnp.minimum((i + 1) * per, nhalo - 1), col0 + c))
    out_spec = pl.BlockSpec((nres, rows, C_OUT_W), lambda r, i: (r, i, 0))
    out_sds = jax.ShapeDtypeStruct((dil, n, C_OUT_W), F32)
    return pl.pallas_call(
        functools.partial(_dil_kernel, dil=dil, half=half, rows=rows, n_p=n_p, sub_p=sub_p, sub_s=sub_s,
                          slopes=tuple(slopes[p_idx * C_PER:(p_idx + 1) * C_PER])),
        grid=(dil // nres, n // rows),
        in_specs=[main(0), prev(1), main(1), nxt(1), prev(2), main(2), nxt(2)],
        out_specs=[out_spec, out_spec],
        out_shape=[out_sds, out_sds],
        compiler_params=_params(2),
        name=f"dil_attn_p{p_idx}",
    )(*([qkv] * 7))


def _dil_combine_kernel(*refs):
    srcs, out_ref, scr = refs[:2 * N_PAT], refs[2 * N_PAT], list(refs[2 * N_PAT + 1:])
    vals = []
    for src in srcs:
        dil = src.shape[0]
        if dil == 1:
            vals.append(src[0])
            continue
        nat = scr.pop(0)
        nchunk = nat.shape[0]
        for r in range(dil):
            for c in range(nchunk):
                nat[c, pl.ds(r, src.shape[1], stride=dil), :] = src[r, :, c * LANE:(c + 1) * LANE]
        vals.append(jnp.concatenate([nat[c] for c in range(nchunk)], axis=-1))
    outs, lses = vals[:N_PAT], vals[N_PAT:]
    m = functools.reduce(jnp.maximum, lses)
    es = [jnp.exp(l - m) for l in lses]
    num = sum(e * o for e, o in zip(es, outs))
    out_ref[...] = (num / sum(es)).astype(out_ref.dtype)


def dil_combine(outs, lses, tr=512):
    t = outs[0].shape[0] * outs[0].shape[1]
    tr = _pick(t, tr, 8 * max(d for _, d in C_PATTERNS))
    spec = lambda a: pl.BlockSpec((a.shape[0], tr // a.shape[0], C_OUT_W), lambda i: (0, i, 0))
    n_scr = sum(2 for a in outs if a.shape[0] > 1)
    return pl.pallas_call(
        _dil_combine_kernel,
        grid=(t // tr,),
        in_specs=[spec(a) for a in outs + lses],
        out_specs=pl.BlockSpec((tr, C_OUT_W), lambda i: (i, 0)),
        out_shape=jax.ShapeDtypeStruct((t, C_OUT_W), BF16),
        scratch_shapes=[pltpu.VMEM((C_OUT_W // LANE, tr, LANE), F32)] * n_scr,
        compiler_params=_params(1),
        name="dil_combine",
    )(*outs, *lses)


def _xattn_block_kernel(o_ref, x_ref, gmix_ref, gpre_ref, wq_ref, kv_ref, wo_ref, gpost_ref, gnext_ref,
                        xo_ref, h_ref):
    xn = x_ref[...] + _rms(o_ref[...].astype(F32), gmix_ref[...])
    q = _dot(_rms(xn, gpre_ref[...]).astype(BF16), wq_ref[...]).astype(BF16)
    heads = []
    for h in range(X_HEADS):
        sl = slice(h * HEAD, (h + 1) * HEAD)
        s = _dot_nt(q[:, sl], kv_ref[:, sl]) * SCALE
        p = jnp.exp(s - jnp.max(s, axis=-1, keepdims=True))
        l = jnp.sum(p, axis=-1, keepdims=True)
        o = _dot(p.astype(BF16), kv_ref[:, X_W + h * HEAD:X_W + (h + 1) * HEAD]) * (1.0 / l)
        heads.append(o.astype(BF16))
    xn = xn + _rms(_dot(jnp.concatenate(heads, axis=-1), wo_ref[...]), gpost_ref[...])
    xo_ref[...] = xn
    h_ref[...] = _rms(xn, gnext_ref[...]).astype(h_ref.dtype)


def xattn_block(o, x, g_mix_post, g_pre, w_xq, kv, w_xo, g_post, g_next, layer, groups, n_mem, tm=256):
    t, d = x.shape
    (_, bp, sp), (_, _, ss) = groups
    tm = _pick(math.gcd(sp, ss), tm, 8)
    n_p = bp * sp // tm

    def seq_of(i):
        return jnp.where(i < n_p, i // (sp // tm), bp + (i - n_p) // (ss // tm))

    row = pl.BlockSpec((tm, d), lambda i: (i, 0))
    vec = _wspec(layer, (1, d), lambda i: (0, 0))
    g3 = lambda g: g.reshape(g.shape[0], 1, d)
    return pl.pallas_call(
        _xattn_block_kernel,
        grid=(t // tm,),
        in_specs=[row, row, vec, vec,
                  _wspec(layer, (d, X_W), lambda i: (0, 0)),
                  pl.BlockSpec((n_mem, 2 * X_W), lambda i: (seq_of(i), 0)),
                  _wspec(layer, (X_W, d), lambda i: (0, 0)), vec, vec],
        out_specs=[row, row],
        out_shape=[jax.ShapeDtypeStruct((t, d), F32), jax.ShapeDtypeStruct((t, d), BF16)],
        compiler_params=_params(1),
        name="xattn_block",
    )(o, x, g3(g_mix_post), g3(g_pre), w_xq, kv, w_xo, g3(g_post), g3(g_next))


def _rope_tables(seq_len):
    t = jnp.arange(seq_len)
    row = (t // GRID_W).astype(F32)
    col = (t % GRID_W).astype(F32)
    n_pairs = HEAD // 4
    inv_freq = ROPE_BASE ** (-jnp.arange(n_pairs, dtype=F32) / n_pairs)
    ang = jnp.concatenate([row[:, None] * inv_freq[None, :], col[:, None] * inv_freq[None, :]], axis=-1)
    cos = jnp.repeat(jnp.cos(ang), 2, axis=-1)
    sin = jnp.repeat(jnp.sin(ang), 2, axis=-1)
    even = (jnp.arange(HEAD) % 2 == 0)[None, :]
    return cos, jnp.where(even, -sin, 0.0), jnp.where(even, 0.0, sin)


def _alibi_slopes():
    h = np.arange(1, C_HEADS + 1, dtype=np.float32)
    return [float(v) for v in np.float32(2.0) ** (-np.float32(ALIBI_MAX_EXP) * h / np.float32(C_HEADS))]


def kernel(x_prompt, x_sample, mem_prompt, mem_sample, ffn1_pre_norm, ffn1_w_in, ffn1_w_out, ffn1_post_norm, mix_pre_norm, w_mix_in, a_q_norm, a_k_norm, w_a_proj, w_b_proj, w_c_proj, w_branch_gate, b_branch_gate, w_mix_out, mix_post_norm, xattn_pre_norm, mem_norm, w_xq, w_xkv, w_xo, xattn_post_norm, ffn2_pre_norm, ffn2_w_in, ffn2_w_out, ffn2_post_norm):
    bp, sp, d = x_prompt.shape
    bs, ss, _ = x_sample.shape
    n_mem = mem_prompt.shape[1]
    depth = ffn1_w_in.shape[0]
    tp = bp * sp
    groups = ((0, bp, sp), (tp, bs, ss))
    assert tp % ss == 0 and sp % GRID_W == 0 and ss % GRID_W == 0

    x = (x_prompt.reshape(tp, d), x_sample.reshape(bs * ss, d))
    mem = jnp.concatenate([mem_prompt.reshape(bp * n_mem, d), mem_sample.reshape(bs * n_mem, d)], axis=0)

    rope = _rope_tables(max(sp, ss))
    dft = {s: _dft_tables(s) for s in {sp, ss}}
    kc = np.arange(HEAD)
    ang128 = 2.0 * np.pi * ((kc[:, None] * kc[None, :]) % HEAD) / HEAD
    cs128 = jnp.asarray(np.concatenate([np.cos(ang128), np.sin(ang128)], axis=1), BF16)
    slopes = _alibi_slopes()
    head_scale = jnp.concatenate([jnp.full((A_Q_HEADS, HEAD), SCALE * math.log2(math.e), F32),
                                  jnp.ones((A_KV_HEADS, HEAD), F32)], axis=0)
    qk_gains = jnp.concatenate([jnp.repeat(a_q_norm[:, None, :], A_Q_HEADS, axis=1),
                                jnp.repeat(a_k_norm[:, None, :], A_KV_HEADS, axis=1)], axis=1)

    bf = lambda w: w.astype(BF16)
    w_mix_out = bf(w_mix_out)
    w_a_proj, w_b_proj, w_c_proj = bf(w_a_proj), bf(w_b_proj), bf(w_c_proj)
    w_xq, w_xkv, w_xo = bf(w_xq), bf(w_xkv), bf(w_xo)

    nat_block = lambda j: jnp.where(j < NAT_V, BLK_B + j,
                                    jnp.where(j == NAT_V, BLK_V, BLK_CQ + N_PAT * (j - NAT_C)))

    h = rms_rows(x, ffn1_pre_norm, 0)
    for l in range(depth):
        act, w_out = swiglu_in(h, ffn1_w_in, l, riders=[ffn1_w_out])
        o, w_gate, w_min = matmul(act, w_out[None], 0, BF16, tm=512, tn=512, name="ffn_out",
                                  riders=[w_branch_gate, w_mix_in], rider_layer=l)
        w_gate, w_min = w_gate[None], w_min[None]
        x, u = resid_norm(o, x, ffn1_post_norm, l, mix_pre_norm, l, 0.5, tp)

        qk = mix_qk(u, w_min, 0, qk_gains, l, head_scale, rope, groups)
        nat = matmul(u, w_min, 0, BF16, tn=C_OUT_W, n_out=NAT_W, col_block=nat_block, name="mix_nat")
        y_a = y_b = None
        for g in groups:
            y_a = attn_a(qk, nat, g, y_a)
        zc, zs = chan_dft(nat, cs128)
        for g in groups:
            y_b = seq_dft(zc, zs, dft[g[2]], g, y_b)
        outs, lses = [], []
        for p_idx, (_, dil) in enumerate(C_PATTERNS):
            if dil == 1:
                qkv, col0 = nat.reshape(1, *nat.shape), NAT_C
            else:
                qkv, col0 = mix_dil(u, w_min, 0, p_idx), 0
            o_p, l_p = dil_attn(qkv, col0, p_idx, groups, slopes)
            outs.append(o_p)
            lses.append(l_p)
        y_c = dil_combine(outs, lses)
        merged = merge_branches(u, y_a, y_b, y_c, w_gate, 0, b_branch_gate,
                                w_a_proj, w_b_proj, w_c_proj, l)
        o = matmul(merged, w_mix_out, l, BF16, name="mix_out")
        kv = matmul(rms_rows(mem, mem_norm, l), w_xkv, l, BF16, name="xattn_kv")
        x, h = xattn_block(o, x, mix_post_norm, xattn_pre_norm, w_xq, kv, w_xo, xattn_post_norm,
                           ffn2_pre_norm, l, groups, n_mem)

        act, w_out = swiglu_in(h, ffn2_w_in, l, riders=[ffn2_w_out])
        o = matmul(act, w_out[None], 0, BF16, tm=512, tn=512, name="ffn_out")
        if l + 1 < depth:
            x, h = resid_norm(o, x, ffn2_post_norm, l, ffn1_pre_norm, l + 1, 0.5, tp)
        else:
            y_p, y_s = resid_norm(o, x, ffn2_post_norm, l, None, None, 0.5, tp)

    return y_p.reshape(bp, sp, d), y_s.reshape(bs, ss, d)
```

```python
import functools
import math

import numpy as np
import jax
import jax.numpy as jnp
from jax import lax
from jax.experimental import pallas as pl
from jax.experimental.pallas import tpu as pltpu

F32 = jnp.float32
BF16 = jnp.bfloat16

HEAD = 128
GRID_W = 64
EPS = 1e-6
ROPE_BASE = 10000.0
A_Q_HEADS = 12
A_KV_HEADS = 4
A_GROUPS = A_Q_HEADS // A_KV_HEADS
B_GROUPS = 20
C_PATTERNS = ((128, 1), (512, 4), (2048, 16))
C_PER = 4
C_HEADS = C_PER * len(C_PATTERNS)
ALIBI_MAX_EXP = 8.0
X_HEADS = 4
N_BRANCH = 3

A_Q_W = A_Q_HEADS * HEAD
A_KV_W = A_KV_HEADS * HEAD
QK_W = A_Q_W + A_KV_W
B_W = B_GROUPS * HEAD
C_W = C_HEADS * HEAD
C_OUT_W = C_PER * HEAD
X_W = X_HEADS * HEAD
MIX_W = A_Q_W + 2 * A_KV_W + B_W + 3 * C_W
BLK_V = QK_W // C_OUT_W
BLK_CQ = (QK_W + A_KV_W + B_W) // C_OUT_W
N_PAT = len(C_PATTERNS)
BLK_B = (QK_W + A_KV_W) // C_OUT_W
NAT_W = B_W + A_KV_W + 3 * C_OUT_W
NAT_V = B_W // C_OUT_W
NAT_C = NAT_V + 1
SCALE = HEAD ** -0.5
NEG = -1e30
HALO = HEAD // 2

VMEM_LIMIT_V7X = 56 * 1024 * 1024
LANE = 128


def _params(n_axes):
    return pltpu.CompilerParams(dimension_semantics=("arbitrary",) * n_axes,
                                vmem_limit_bytes=VMEM_LIMIT_V7X)


def _pick(n, pref, unit=LANE):
    if n <= pref:
        return n
    best = None
    for t in range(unit, pref + 1, unit):
        if n % t == 0:
            best = t
    assert best is not None, (n, pref)
    return best


def _wspec(layer, shape, imap):
    return pl.BlockSpec((None,) + shape, lambda *g: (layer,) + tuple(imap(*g)))


def _rms(x, g):
    return x * lax.rsqrt(jnp.mean(x * x, axis=-1, keepdims=True) + EPS) * g


def _sigmoid(x):
    return 1.0 / (1.0 + jnp.exp(-x))


def _dot(a, b):
    return jnp.dot(a, b, preferred_element_type=F32)


def _dot_nt(a, b):
    return lax.dot_general(a, b, (((1,), (1,)), ((), ())), preferred_element_type=F32)


def _row_inputs(x, tr):
    if not isinstance(x, tuple):
        return [x], [pl.BlockSpec((tr, x.shape[1]), lambda i: (i, 0))], None
    xp, xs = x
    n_p = xp.shape[0] // tr
    d = xp.shape[1]
    return ([xp, xs],
            [pl.BlockSpec((tr, d), lambda i: (jnp.minimum(i, n_p - 1), 0)),
             pl.BlockSpec((tr, d), lambda i: (jnp.maximum(i - n_p, 0), 0))], n_p)


def _load_rows(x_refs, n_p):
    if len(x_refs) == 1:
        return x_refs[0][...]
    return jnp.where(pl.program_id(0) < n_p, x_refs[0][...], x_refs[1][...])


def _rms_kernel(*refs, n_p):
    x_refs, g_ref, o_ref = refs[:-2], refs[-2], refs[-1]
    o_ref[...] = _rms(_load_rows(x_refs, n_p), g_ref[...]).astype(o_ref.dtype)


def rms_rows(x, g, layer, tr=256):
    pieces = x if isinstance(x, tuple) else (x,)
    t = sum(a.shape[0] for a in pieces)
    d = g.shape[-1]
    tr = _pick(math.gcd(*[a.shape[0] for a in pieces]), tr, 8)
    arrs, specs, n_p = _row_inputs(x, tr)
    return pl.pallas_call(
        functools.partial(_rms_kernel, n_p=n_p),
        grid=(t // tr,),
        in_specs=specs + [_wspec(layer, (1, d), lambda i: (0, 0))],
        out_specs=pl.BlockSpec((tr, d), lambda i: (i, 0)),
        out_shape=jax.ShapeDtypeStruct((t, d), BF16),
        compiler_params=_params(1),
        name="rms_rows",
    )(*arrs, g.reshape(g.shape[0], 1, d))


def _resid_kernel(*refs, coef, n_x, n_p, has_next):
    o_ref, x_refs, gp_ref, rest = refs[0], refs[1:1 + n_x], refs[1 + n_x], refs[2 + n_x:]
    xn = _load_rows(x_refs, n_p) + coef * _rms(o_ref[...].astype(F32), gp_ref[...])
    if has_next:
        gn_ref, xo_ref, h_ref = rest
        xo_ref[...] = xn
        h_ref[...] = _rms(xn, gn_ref[...]).astype(h_ref.dtype)
    else:
        yp_ref, ys_ref = rest
        i = pl.program_id(0)

        @pl.when(i < n_p)
        def _():
            yp_ref[...] = xn

        @pl.when(i >= n_p)
        def _():
            ys_ref[...] = xn


def resid_norm(o, x, g_post, layer, g_next, next_layer, coef, rows_p, tr=256):
    t, d = o.shape
    tr = _pick(math.gcd(rows_p, t - rows_p), tr, 8)
    arrs, specs, n_p = _row_inputs(x, tr)
    row = pl.BlockSpec((tr, d), lambda i: (i, 0))
    vec = lambda lyr: _wspec(lyr, (1, d), lambda i: (0, 0))
    g3 = lambda g: g.reshape(g.shape[0], 1, d)
    if g_next is None:
        n_p = rows_p // tr
        return pl.pallas_call(
            functools.partial(_resid_kernel, coef=coef, n_x=len(arrs), n_p=n_p, has_next=False),
            grid=(t // tr,),
            in_specs=[row] + specs + [vec(layer)],
            out_specs=[pl.BlockSpec((tr, d), lambda i: (jnp.minimum(i, n_p - 1), 0)),
                       pl.BlockSpec((tr, d), lambda i: (jnp.maximum(i - n_p, 0), 0))],
            out_shape=[jax.ShapeDtypeStruct((rows_p, d), F32), jax.ShapeDtypeStruct((t - rows_p, d), F32)],
            compiler_params=_params(1),
            name="resid_last",
        )(o, *arrs, g3(g_post))
    return pl.pallas_call(
        functools.partial(_resid_kernel, coef=coef, n_x=len(arrs), n_p=n_p, has_next=True),
        grid=(t // tr,),
        in_specs=[row] + specs + [vec(layer), vec(next_layer)],
        out_specs=[row, row],
        out_shape=[jax.ShapeDtypeStruct((t, d), F32), jax.ShapeDtypeStruct((t, d), BF16)],
        compiler_params=_params(1),
        name="resid_norm",
    )(o, *arrs, g3(g_post), g3(g_next))


BF16_ROWS = 16


def _with_riders(body, n_in, n_out, n_riders):
    def kern(*refs):
        ins, r_in = refs[:n_in], refs[n_in:n_in + n_riders]
        outs = refs[n_in + n_riders:n_in + n_riders + n_out]
        r_out = refs[n_in + n_riders + n_out:n_in + 2 * n_riders + n_out]
        body(*ins, *outs, *refs[n_in + 2 * n_riders + n_out:])
        for src, dst in zip(r_in, r_out):
            dst[...] = src[...].astype(dst.dtype)
    return kern


def _rider_specs(riders, layer, grid):
    n_steps = grid[0] * grid[1]
    in_specs, out_specs, out_shapes = [], [], []
    for w in riders:
        _, k, n = w.shape
        assert k % (n_steps * BF16_ROWS) == 0, (w.shape, grid)
        slab = (k // n_steps, n)
        step = lambda i, j: (i * grid[1] + j, 0)
        in_specs.append(_wspec(layer, slab, step))
        out_specs.append(pl.BlockSpec(slab, step))
        out_shapes.append(jax.ShapeDtypeStruct((k, n), BF16))
    return in_specs, out_specs, out_shapes


def _mm_kernel(x_ref, w_ref, o_ref):
    o_ref[...] = _dot(x_ref[...], w_ref[...]).astype(o_ref.dtype)


def matmul(x, w, layer, out_dtype, tm=1024, tn=1024, n_out=None, col_block=None, name="matmul",
           riders=(), rider_layer=None):
    m, k = x.shape
    n = w.shape[2] if n_out is None else n_out
    tm = _pick(m, tm, 8)
    tn = _pick(n, tn)
    col_block = col_block or (lambda j: j)
    grid = (m // tm, n // tn)
    r_in, r_out, r_shapes = _rider_specs(riders, rider_layer, grid)
    out = pl.pallas_call(
        _with_riders(_mm_kernel, 2, 1, len(riders)),
        grid=grid,
        in_specs=[pl.BlockSpec((tm, k), lambda i, j: (i, 0)),
                  _wspec(layer, (k, tn), lambda i, j: (0, col_block(j)))] + r_in,
        out_specs=[pl.BlockSpec((tm, tn), lambda i, j: (i, j))] + r_out,
        out_shape=[jax.ShapeDtypeStruct((m, n), out_dtype)] + r_shapes,
        compiler_params=_params(2),
        name=name,
    )(x, w, *riders)
    return out if riders else out[0]


SWIGLU_PARTS = 2


def _swiglu_kernel(x_ref, wg_ref, wu_ref, o_ref):
    wg = wg_ref[...].astype(x_ref.dtype)
    wu = wu_ref[...].astype(x_ref.dtype)
    part = x_ref.shape[0] // SWIGLU_PARTS
    for p in range(SWIGLU_PARTS):
        rows = slice(p * part, (p + 1) * part)
        x = x_ref[rows, :]
        g = _dot(x, wg)
        u = _dot(x, wu)
        o_ref[rows, :] = (g * _sigmoid(g) * u).astype(o_ref.dtype)


def swiglu_in(h, w_in, layer, tm=2048, tn=256, riders=()):
    t, d = h.shape
    f = w_in.shape[2] // 2
    tm = _pick(t, tm, 8)
    tn = _pick(f, tn)
    nj = f // tn
    grid = (t // tm, nj)
    r_in, r_out, r_shapes = _rider_specs(riders, layer, grid)
    return pl.pallas_call(
        _with_riders(_swiglu_kernel, 3, 1, len(riders)),
        grid=grid,
        in_specs=[pl.BlockSpec((tm, d), lambda i, j: (i, 0)),
                  _wspec(layer, (d, tn), lambda i, j: (0, j)),
                  _wspec(layer, (d, tn), lambda i, j: (0, nj + j))] + r_in,
        out_specs=[pl.BlockSpec((tm, tn), lambda i, j: (i, j))] + r_out,
        out_shape=[jax.ShapeDtypeStruct((t, f), BF16)] + r_shapes,
        compiler_params=_params(2),
        name="swiglu_in",
    )(h, w_in, w_in, *riders)


def _merge_kernel(u_ref, a_ref, b_ref, c_ref, wg0_ref, wg1_ref, wg2_ref, bg0_ref, bg1_ref, bg2_ref,
                  wa_ref, wb_ref, wc_ref, o_ref):
    u = u_ref[...]
    acc = _sigmoid(_dot(u, wg0_ref[...]) + bg0_ref[...]) * _dot(a_ref[...], wa_ref[...])
    acc += _sigmoid(_dot(u, wg1_ref[...]) + bg1_ref[...]) * _dot(b_ref[...], wb_ref[...])
    acc += _sigmoid(_dot(u, wg2_ref[...]) + bg2_ref[...]) * _dot(c_ref[...], wc_ref[...])
    o_ref[...] = acc.astype(o_ref.dtype)


def merge_branches(u, a, b, c, w_gate, gate_layer, b_gate, w_a, w_b, w_c, layer, tm=512, tn=512):
    t, d = u.shape
    tm = _pick(t, tm, 8)
    tn = _pick(d, tn)
    nj = d // tn
    act = lambda width: pl.BlockSpec((tm, width), lambda i, j: (i, 0), pipeline_mode=pl.Buffered(1))
    gate_w = lambda br: _wspec(gate_layer, (d, tn), lambda i, j: (0, br * nj + j))
    gate_b = lambda br: _wspec(layer, (1, tn), lambda i, j: (0, br * nj + j))
    proj_w = lambda width: _wspec(layer, (width, tn), lambda i, j: (0, j))
    bg = b_gate.reshape(b_gate.shape[0], 1, N_BRANCH * d)
    return pl.pallas_call(
        _merge_kernel,
        grid=(t // tm, nj),
        in_specs=[act(d), act(A_Q_W), act(B_W), act(C_OUT_W),
                  gate_w(0), gate_w(1), gate_w(2), gate_b(0), gate_b(1), gate_b(2),
                  proj_w(A_Q_W), proj_w(B_W), proj_w(C_OUT_W)],
        out_specs=pl.BlockSpec((tm, tn), lambda i, j: (i, j)),
        out_shape=jax.ShapeDtypeStruct((t, d), BF16),
        compiler_params=_params(2),
        name="merge_branches",
    )(u, a, b, c, w_gate, w_gate, w_gate, bg, bg, bg, w_a, w_b, w_c)


QK_PARTS = 4


def _mix_qk_kernel(u_ref, w_ref, g_ref, sc_ref, cs_ref, sa_ref, sb_ref, o_ref):
    w = w_ref[...]
    part = u_ref.shape[0] // QK_PARTS
    for p in range(QK_PARTS):
        rows = slice(p * part, (p + 1) * part)
        y = _dot(u_ref[rows, :], w)
        cs, sa, sb = cs_ref[rows, :], sa_ref[rows, :], sb_ref[rows, :]
        for h in range(o_ref.shape[1] // HEAD):
            sl = slice(h * HEAD, (h + 1) * HEAD)
            x = _rms(y[:, sl], g_ref[h:h + 1, :])
            r = x * cs + pltpu.roll(x, HEAD - 1, 1) * sa + pltpu.roll(x, 1, 1) * sb
            o_ref[rows, sl] = (r * sc_ref[h:h + 1, :]).astype(o_ref.dtype)


def mix_qk(u, w_mix_in, layer, gains, gain_layer, scales, rope, groups, tm=1024, tn=1024):
    t, d = u.shape
    (_, bp, sp), (_, _, ss) = groups
    tm = _pick(math.gcd(sp, ss), tm, 8)
    tn = _pick(QK_W, tn)
    n_p = bp * sp // tm
    hpt = tn // HEAD

    def pos_block(i):
        return jnp.where(i < n_p, i % (sp // tm), (i - n_p) % (ss // tm))

    tab = pl.BlockSpec((tm, HEAD), lambda i, j: (pos_block(i), 0))
    return pl.pallas_call(
        _mix_qk_kernel,
        grid=(t // tm, QK_W // tn),
        in_specs=[pl.BlockSpec((tm, d), lambda i, j: (i, 0)),
                  _wspec(layer, (d, tn), lambda i, j: (0, j)),
                  _wspec(gain_layer, (hpt, HEAD), lambda i, j: (j, 0)),
                  pl.BlockSpec((hpt, HEAD), lambda i, j: (j, 0)),
                  tab, tab, tab],
        out_specs=pl.BlockSpec((tm, tn), lambda i, j: (i, j)),
        out_shape=jax.ShapeDtypeStruct((t, QK_W), BF16),
        compiler_params=_params(2),
        name="mix_qk",
    )(u, w_mix_in, gains, scales, *rope)


def _mix_dil_kernel(u_ref, w_ref, o_ref, scr_ref, *, dil):
    w = w_ref[...]
    nchunk, tm, _ = scr_ref.shape
    part = max(tm // QK_PARTS, BF16_ROWS * dil)
    for p in range(tm // part):
        y = _dot(u_ref[p * part:(p + 1) * part, :], w)
        sub = slice(p * part // dil, (p + 1) * part // dil)
        for c in range(nchunk):
            scr_ref[c, p * part:(p + 1) * part, :] = y[:, c * LANE:(c + 1) * LANE]
            for r in range(dil):
                o_ref[r, sub, c * LANE:(c + 1) * LANE] = (
                    scr_ref[c, pl.ds(p * part + r, part // dil, stride=dil), :].astype(o_ref.dtype))


def mix_dil(u, w_mix_in, layer, p_idx, tm=1024):
    t, d = u.shape
    dil = C_PATTERNS[p_idx][1]
    tm = _pick(t, tm, 8 * dil)
    return pl.pallas_call(
        functools.partial(_mix_dil_kernel, dil=dil),
        grid=(t // tm, 3),
        in_specs=[pl.BlockSpec((tm, d), lambda i, j: (i, 0)),
                  _wspec(layer, (d, C_OUT_W), lambda i, j: (0, BLK_CQ + p_idx + N_PAT * j))],
        out_specs=pl.BlockSpec((dil, tm // dil, C_OUT_W), lambda i, j: (0, i, j)),
        out_shape=jax.ShapeDtypeStruct((dil, t // dil, 3 * C_OUT_W), BF16),
        scratch_shapes=[pltpu.VMEM((C_OUT_W // LANE, tm, LANE), F32)],
        compiler_params=_params(2),
        name=f"mix_dil_p{p_idx}",
    )(u, w_mix_in)


def _attn_a_kernel(q_ref, k_ref, v_ref, *rest):
    o_ref = rest[-1]
    k = k_ref[...]
    v = v_ref[...]
    part = q_ref.shape[0] // ATTN_PARTS
    for r in range(ATTN_PARTS):
        rows = slice(r * part, (r + 1) * part)
        for g in range(A_GROUPS):
            sl = slice(g * HEAD, (g + 1) * HEAD)
            s = _dot_nt(q_ref[rows, sl], k)
            p = jnp.exp2(s - jnp.max(s, axis=-1, keepdims=True))
            l = jnp.sum(p, axis=-1, keepdims=True)
            o = _dot(p.astype(BF16), v) * (1.0 / l)
            o_ref[rows, sl] = o.astype(o_ref.dtype)


ATTN_PARTS = 4


def attn_a(qk, nat, group, prev_out, tq=1024):
    row0, nseq, s = group
    t = qk.shape[0]
    tq = _pick(s, tq, 8)
    nq = s // tq
    rb0, sb0 = row0 // tq, row0 // s
    gw = A_GROUPS * HEAD
    in_specs = [pl.BlockSpec((tq, gw), lambda b, h, i: (rb0 + b * nq + i, h)),
                pl.BlockSpec((s, HEAD), lambda b, h, i: (sb0 + b, A_Q_HEADS + h)),
                pl.BlockSpec((s, HEAD), lambda b, h, i: (sb0 + b, NAT_V * (C_OUT_W // HEAD) + h))]
    args = [qk, qk, nat]
    aliases = {}
    if prev_out is not None:
        in_specs.append(pl.BlockSpec(memory_space=pl.ANY))
        args.append(prev_out)
        aliases = {3: 0}
    return pl.pallas_call(
        _attn_a_kernel,
        grid=(nseq, A_KV_HEADS, nq),
        in_specs=in_specs,
        out_specs=pl.BlockSpec((tq, gw), lambda b, h, i: (rb0 + b * nq + i, h)),
        out_shape=jax.ShapeDtypeStruct((t, A_Q_W), BF16),
        input_output_aliases=aliases,
        compiler_params=_params(3),
        name="attn_a",
    )(*args)


RADIX = 8
DFT_TM = 256


def _chan_dft_kernel(z_ref, cs_ref, zc_ref, zs_ref, scr_ref):
    cs = cs_ref[...]
    rows = zc_ref.shape[1]
    for g in range(B_GROUPS):
        sl = slice(g * HEAD, (g + 1) * HEAD)
        y = _dot(z_ref[:, sl], cs) * (HEAD ** -0.5)
        scr_ref[2 * g] = y[:, :HEAD]
        scr_ref[2 * g + 1] = y[:, HEAD:]
        for b in range(RADIX):
            zc_ref[b, :, sl] = scr_ref[2 * g, pl.ds(b, rows, stride=RADIX), :].astype(zc_ref.dtype)
            zs_ref[b, :, sl] = scr_ref[2 * g + 1, pl.ds(b, rows, stride=RADIX), :].astype(zs_ref.dtype)


def chan_dft(nat, cs128, tr=512):
    t = nat.shape[0]
    tr = _pick(t, tr, 16 * RADIX)
    out = pl.BlockSpec((RADIX, tr // RADIX, B_W), lambda i: (0, i, 0))
    return pl.pallas_call(
        _chan_dft_kernel,
        grid=(t // tr,),
        in_specs=[pl.BlockSpec((tr, B_W), lambda i: (i, 0)),
                  pl.BlockSpec((HEAD, 2 * HEAD), lambda i: (0, 0))],
        out_specs=[out, out],
        out_shape=[jax.ShapeDtypeStruct((RADIX, t // RADIX, B_W), BF16)] * 2,
        scratch_shapes=[pltpu.VMEM((2 * B_GROUPS, tr, LANE), F32)],
        compiler_params=_params(1),
        name="chan_dft",
    )(nat, cs128)


def _outer_dft_real(tre, tim, k2):
    sums = {}
    for b in range(RADIX):
        ang = 2.0 * math.pi * ((k2 * b) % RADIX) / RADIX
        for coef, tile in ((math.cos(ang), tre[b]), (math.sin(ang), tim[b])):
            mag = round(abs(coef), 9)
            if mag == 0.0:
                continue
            term = tile if coef > 0 else -tile
            sums[mag] = term if mag not in sums else sums[mag] + term
    return sum(v if mag == 1.0 else v * mag for mag, v in sums.items())


def _seq_dft_kernel(m_ref, zc_ref, zs_ref, *rest, scale):
    o_ref = rest[-1]
    a = zc_ref.shape[1]
    tm = m_ref.shape[1] // 2
    tre, tim = [], []
    for b in range(RADIX):
        t = _dot(m_ref[b, :, :a], zc_ref[b]) + _dot(m_ref[b, :, a:], zs_ref[b])
        tre.append(t[:tm])
        tim.append(t[tm:])
    for k2 in range(RADIX):
        o_ref[k2] = (_outer_dft_real(tre, tim, k2) * scale).astype(o_ref.dtype)


def seq_dft(zc, zs, tables, group, prev_out, tn=256):
    row0, nseq, s = group
    t = zc.shape[1] * RADIX
    a = s // RADIX
    ni, tm2 = tables.shape[1], tables.shape[2]
    tm = tm2 // 2
    tn = _pick(B_W, tn)
    nj = B_W // tn
    sb0 = row0 // s
    zspec = pl.BlockSpec((RADIX, a, tn), lambda i, b, j: (0, sb0 + b, j))
    in_specs = [pl.BlockSpec((RADIX, None, tm2, 2 * a), lambda i, b, j: (0, i, 0, 0)), zspec, zspec]
    args = [tables, zc, zs]
    aliases = {}
    out_shape = (t // a, ni, tm, B_W)
    if prev_out is not None:
        in_specs.append(pl.BlockSpec(memory_space=pl.ANY))
        args.append(prev_out.reshape(out_shape))
        aliases = {3: 0}
    return pl.pallas_call(
        functools.partial(_seq_dft_kernel, scale=s ** -0.5),
        grid=(ni, nseq, nj),
        in_specs=in_specs,
        out_specs=pl.BlockSpec((RADIX, None, tm, tn), lambda i, b, j: (sb0 + b, i, 0, j)),
        out_shape=jax.ShapeDtypeStruct(out_shape, BF16),
        input_output_aliases=aliases,
        compiler_params=_params(3),
        name="seq_dft",
    )(*args).reshape(t, B_W)


def _dft_tables(s):
    a = s // RADIX
    tm = min(a, DFT_TM)
    k1 = jnp.arange(a, dtype=jnp.int32)
    n = RADIX * jnp.arange(a, dtype=jnp.int32)[None, :] + jnp.arange(RADIX, dtype=jnp.int32)[:, None]
    ang = ((k1[None, :, None] * n[:, None, :]) % s).astype(F32) * (2.0 * math.pi / s)
    c, sn = jnp.cos(ang), jnp.sin(ang)
    tile = lambda m: m.reshape(RADIX, a // tm, tm, 2 * a)
    re = tile(jnp.concatenate([c, -sn], axis=-1))
    im = tile(jnp.concatenate([-sn, -c], axis=-1))
    return jnp.concatenate([re, im], axis=2).astype(BF16)


def _dil_kernel(q_ref, kp_ref, km_ref, kn_ref, vp_ref, vm_ref, vn_ref, o_ref, l_ref, *,
                dil, half, rows, n_p, sub_p, sub_s, slopes):
    i = pl.program_id(1)
    start = i * rows
    in_p = start < n_p
    seq_len = jnp.where(in_p, sub_p, sub_s)
    local = jnp.where(in_p, start % sub_p, (start - n_p) % sub_s)
    qi = lax.broadcasted_iota(jnp.int32, (HEAD, 2 * HEAD), 0)
    ci = lax.broadcasted_iota(jnp.int32, (HEAD, 2 * HEAD), 1)
    rel = ci - HALO - qi
    band = jnp.abs(rel) <= half
    dist = (jnp.abs(rel) * dil).astype(F32)
    nsub = rows // HEAD

    def window(main_ref, prev_ref, next_ref, r, s, sl):
        lo, hi = s * HEAD - HALO, (s + 1) * HEAD + HALO
        parts = [prev_ref[r, :, sl]] if lo < 0 else []
        parts.append(main_ref[r, max(lo, 0):min(hi, rows), sl])
        if hi > rows:
            parts.append(next_ref[r, :, sl])
        return parts[0] if len(parts) == 1 else jnp.concatenate(parts, axis=0)

    for s in range(nsub):
        kpos = local + s * HEAD - HALO + ci
        valid = band & (kpos >= 0) & (kpos < seq_len)
        qs = slice(s * HEAD, (s + 1) * HEAD)
        for h in range(C_PER):
            sl = slice(h * HEAD, (h + 1) * HEAD)
            bias = jnp.where(valid, -slopes[h] * dist, NEG)
            for r in range(q_ref.shape[0]):
                sc = _dot_nt(q_ref[r, qs, sl], window(km_ref, kp_ref, kn_ref, r, s, sl))
                sc = sc * SCALE + bias
                m = jnp.max(sc, axis=-1, keepdims=True)
                e = jnp.exp(sc - m)
                den = jnp.sum(e, axis=-1, keepdims=True)
                p = (e * (1.0 / den)).astype(BF16)
                o_ref[r, qs, sl] = _dot(p, window(vm_ref, vp_ref, vn_ref, r, s, sl))
                l_ref[r, qs, sl] = jnp.broadcast_to(m + jnp.log(den), (HEAD, HEAD))


def dil_attn(qkv, col0, p_idx, groups, slopes, rows_per_step=512):
    window, dil = C_PATTERNS[p_idx]
    half = window // (2 * dil)
    assert half <= HALO
    (_, bp, sp), (_, bs, ss) = groups
    n = qkv.shape[1]
    n_p, sub_p, sub_s = bp * sp // dil, sp // dil, ss // dil
    assert sub_p % HEAD == 0 and sub_s % HEAD == 0
    rows = _pick(math.gcd(sub_p, sub_s), rows_per_step)
    per = rows // HALO
    nhalo = n // HALO
    nres = math.gcd(dil, max(1, rows_per_step // rows))

    main = lambda c: pl.BlockSpec((nres, rows, C_OUT_W), lambda r, i: (r, i, col0 + c))
    prev = lambda c: pl.BlockSpec((nres, HALO, C_OUT_W),
                                  lambda r, i: (r, jnp.maximum(i * per - 1, 0), col0 + c))
    nxt = lambda c: pl.BlockSpec((nres, HALO, C_OUT_W),
                                 lambda r, i: (r, jnp.minimum((i + 1) * per, nhalo - 1), col0 + c))
    out_spec = pl.BlockSpec((nres, rows, C_OUT_W), lambda r, i: (r, i, 0))
    out_sds = jax.ShapeDtypeStruct((dil, n, C_OUT_W), F32)
    return pl.pallas_call(
        functools.partial(_dil_kernel, dil=dil, half=half, rows=rows, n_p=n_p, sub_p=sub_p, sub_s=sub_s,
                          slopes=tuple(slopes[p_idx * C_PER:(p_idx + 1) * C_PER])),
        grid=(dil // nres, n // rows),
        in_specs=[main(0), prev(1), main(1), nxt(1), prev(2), main(2), nxt(2)],
        out_specs=[out_spec, out_spec],
        out_shape=[out_sds, out_sds],
        compiler_params=_params(2),
        name=f"dil_attn_p{p_idx}",
    )(*([qkv] * 7))


def _dil_combine_kernel(*refs):
    srcs, out_ref, scr = refs[:2 * N_PAT], refs[2 * N_PAT], list(refs[2 * N_PAT + 1:])
    vals = []
    for src in srcs:
        dil = src.shape[0]
        if dil == 1:
            vals.append(src[0])
            continue
        nat = scr.pop(0)
        nchunk = nat.shape[0]
        for r in range(dil):
            for c in range(nchunk):
                nat[c, pl.ds(r, src.shape[1], stride=dil), :] = src[r, :, c * LANE:(c + 1) * LANE]
        vals.append(jnp.concatenate([nat[c] for c in range(nchunk)], axis=-1))
    outs, lses = vals[:N_PAT], vals[N_PAT:]
    m = functools.reduce(jnp.maximum, lses)
    es = [jnp.exp(l - m) for l in lses]
    num = sum(e * o for e, o in zip(es, outs))
    out_ref[...] = (num / sum(es)).astype(out_ref.dtype)


def dil_combine(outs, lses, tr=512):
    t = outs[0].shape[0] * outs[0].shape[1]
    tr = _pick(t, tr, 8 * max(d for _, d in C_PATTERNS))
    spec = lambda a: pl.BlockSpec((a.shape[0], tr // a.shape[0], C_OUT_W), lambda i: (0, i, 0))
    n_scr = sum(2 for a in outs if a.shape[0] > 1)
    return pl.pallas_call(
        _dil_combine_kernel,
        grid=(t // tr,),
        in_specs=[spec(a) for a in outs + lses],
        out_specs=pl.BlockSpec((tr, C_OUT_W), lambda i: (i, 0)),
        out_shape=jax.ShapeDtypeStruct((t, C_OUT_W), BF16),
        scratch_shapes=[pltpu.VMEM((C_OUT_W // LANE, tr, LANE), F32)] * n_scr,
        compiler_params=_params(1),
        name="dil_combine",
    )(*outs, *lses)


def _xattn_block_kernel(o_ref, x_ref, gmix_ref, gpre_ref, wq_ref, kv_ref, wo_ref, gpost_ref, gnext_ref,
                        xo_ref, h_ref):
    xn = x_ref[...] + _rms(o_ref[...].astype(F32), gmix_ref[...])
    q = _dot(_rms(xn, gpre_ref[...]).astype(BF16), wq_ref[...]).astype(BF16)
    heads = []
    for h in range(X_HEADS):
        sl = slice(h * HEAD, (h + 1) * HEAD)
        s = _dot_nt(q[:, sl], kv_ref[:, sl]) * SCALE
        p = jnp.exp(s - jnp.max(s, axis=-1, keepdims=True))
        l = jnp.sum(p, axis=-1, keepdims=True)
        o = _dot(p.astype(BF16), kv_ref[:, X_W + h * HEAD:X_W + (h + 1) * HEAD]) * (1.0 / l)
        heads.append(o.astype(BF16))
    xn = xn + _rms(_dot(jnp.concatenate(heads, axis=-1), wo_ref[...]), gpost_ref[...])
    xo_ref[...] = xn
    h_ref[...] = _rms(xn, gnext_ref[...]).astype(h_ref.dtype)


def xattn_block(o, x, g_mix_post, g_pre, w_xq, kv, w_xo, g_post, g_next, layer, groups, n_mem, tm=256):
    t, d = x.shape
    (_, bp, sp), (_, _, ss) = groups
    tm = _pick(math.gcd(sp, ss), tm, 8)
    n_p = bp * sp // tm

    def seq_of(i):
        return jnp.where(i < n_p, i // (sp // tm), bp + (i - n_p) // (ss // tm))

    row = pl.BlockSpec((tm, d), lambda i: (i, 0))
    vec = _wspec(layer, (1, d), lambda i: (0, 0))
    g3 = lambda g: g.reshape(g.shape[0], 1, d)
    return pl.pallas_call(
        _xattn_block_kernel,
        grid=(t // tm,),
        in_specs=[row, row, vec, vec,
                  _wspec(layer, (d, X_W), lambda i: (0, 0)),
                  pl.BlockSpec((n_mem, 2 * X_W), lambda i: (seq_of(i), 0)),
                  _wspec(layer, (X_W, d), lambda i: (0, 0)), vec, vec],
        out_specs=[row, row],
        out_shape=[jax.ShapeDtypeStruct((t, d), F32), jax.ShapeDtypeStruct((t, d), BF16)],
        compiler_params=_params(1),
        name="xattn_block",
    )(o, x, g3(g_mix_post), g3(g_pre), w_xq, kv, w_xo, g3(g_post), g3(g_next))


def _rope_tables(seq_len):
    t = jnp.arange(seq_len)
    row = (t // GRID_W).astype(F32)
    col = (t % GRID_W).astype(F32)
    n_pairs = HEAD // 4
    inv_freq = ROPE_BASE ** (-jnp.arange(n_pairs, dtype=F32) / n_pairs)
    ang = jnp.concatenate([row[:, None] * inv_freq[None, :], col[:, None] * inv_freq[None, :]], axis=-1)
    cos = jnp.repeat(jnp.cos(ang), 2, axis=-1)
    sin = jnp.repeat(jnp.sin(ang), 2, axis=-1)
    even = (jnp.arange(HEAD) % 2 == 0)[None, :]
    return cos, jnp.where(even, -sin, 0.0), jnp.where(even, 0.0, sin)


def _alibi_slopes():
    h = np.arange(1, C_HEADS + 1, dtype=np.float32)
    return [float(v) for v in np.float32(2.0) ** (-np.float32(ALIBI_MAX_EXP) * h / np.float32(C_HEADS))]


def kernel(x_prompt, x_sample, mem_prompt, mem_sample, ffn1_pre_norm, ffn1_w_in, ffn1_w_out, ffn1_post_norm, mix_pre_norm, w_mix_in, a_q_norm, a_k_norm, w_a_proj, w_b_proj, w_c_proj, w_branch_gate, b_branch_gate, w_mix_out, mix_post_norm, xattn_pre_norm, mem_norm, w_xq, w_xkv, w_xo, xattn_post_norm, ffn2_pre_norm, ffn2_w_in, ffn2_w_out, ffn2_post_norm):
    bp, sp, d = x_prompt.shape
    bs, ss, _ = x_sample.shape
    n_mem = mem_prompt.shape[1]
    depth = ffn1_w_in.shape[0]
    tp = bp * sp
    groups = ((0, bp, sp), (tp, bs, ss))
    assert tp % ss == 0 and sp % GRID_W == 0 and ss % GRID_W == 0

    x = (x_prompt.reshape(tp, d), x_sample.reshape(bs * ss, d))
    mem = jnp.concatenate([mem_prompt.reshape(bp * n_mem, d), mem_sample.reshape(bs * n_mem, d)], axis=0)

    rope = _rope_tables(max(sp, ss))
    dft = {s: _dft_tables(s) for s in {sp, ss}}
    kc = np.arange(HEAD)
    ang128 = 2.0 * np.pi * ((kc[:, None] * kc[None, :]) % HEAD) / HEAD
    cs128 = jnp.asarray(np.concatenate([np.cos(ang128), np.sin(ang128)], axis=1), BF16)
    slopes = _alibi_slopes()
    head_scale = jnp.concatenate([jnp.full((A_Q_HEADS, HEAD), SCALE * math.log2(math.e), F32),
                                  jnp.ones((A_KV_HEADS, HEAD), F32)], axis=0)
    qk_gains = jnp.concatenate([jnp.repeat(a_q_norm[:, None, :], A_Q_HEADS, axis=1),
                                jnp.repeat(a_k_norm[:, None, :], A_KV_HEADS, axis=1)], axis=1)

    bf = lambda w: w.astype(BF16)
    w_mix_out = bf(w_mix_out)
    w_a_proj, w_b_proj, w_c_proj = bf(w_a_proj), bf(w_b_proj), bf(w_c_proj)
    w_xq, w_xkv, w_xo = bf(w_xq), bf(w_xkv), bf(w_xo)

    nat_block = lambda j: jnp.where(j < NAT_V, BLK_B + j,
                                    jnp.where(j == NAT_V, BLK_V, BLK_CQ + N_PAT * (j - NAT_C)))

    h = rms_rows(x, ffn1_pre_norm, 0)
    for l in range(depth):
        act, w_out = swiglu_in(h, ffn1_w_in, l, riders=[ffn1_w_out])
        o, w_gate, w_min = matmul(act, w_out[None], 0, BF16, tm=512, tn=512, name="ffn_out",
                                  riders=[w_branch_gate, w_mix_in], rider_layer=l)
        w_gate, w_min = w_gate[None], w_min[None]
        x, u = resid_norm(o, x, ffn1_post_norm, l, mix_pre_norm, l, 0.5, tp)

        qk = mix_qk(u, w_min, 0, qk_gains, l, head_scale, rope, groups)
        nat = matmul(u, w_min, 0, BF16, tn=C_OUT_W, n_out=NAT_W, col_block=nat_block, name="mix_nat")
        y_a = y_b = None
        for g in groups:
            y_a = attn_a(qk, nat, g, y_a)
        zc, zs = chan_dft(nat, cs128)
        for g in groups:
            y_b = seq_dft(zc, zs, dft[g[2]], g, y_b)
        outs, lses = [], []
        for p_idx, (_, dil) in enumerate(C_PATTERNS):
            if dil == 1:
                qkv, col0 = nat.reshape(1, *nat.shape), NAT_C
            else:
                qkv, col0 = mix_dil(u, w_min, 0, p_idx), 0
            o_p, l_p = dil_attn(qkv, col0, p_idx, groups, slopes)
            outs.append(o_p)
            lses.append(l_p)
        y_c = dil_combine(outs, lses)
        merged = merge_branches(u, y_a, y_b, y_c, w_gate, 0, b_branch_gate,
                                w_a_proj, w_b_proj, w_c_proj, l)
        o = matmul(merged, w_mix_out, l, BF16, name="mix_out")
        kv = matmul(rms_rows(mem, mem_norm, l), w_xkv, l, BF16, name="xattn_kv")
        x, h = xattn_block(o, x, mix_post_norm, xattn_pre_norm, w_xq, kv, w_xo, xattn_post_norm,
                           ffn2_pre_norm, l, groups, n_mem)

        act, w_out = swiglu_in(h, ffn2_w_in, l, riders=[ffn2_w_out])
        o = matmul(act, w_out[None], 0, BF16, tm=512, tn=512, name="ffn_out")
        if l + 1 < depth:
            x, h = resid_norm(o, x, ffn2_post_norm, l, ffn1_pre_norm, l + 1, 0.5, tp)
        else:
            y_p, y_s = resid_norm(o, x, ffn2_post_norm, l, None, None, 0.5, tp)

    return y_p.reshape(bp, sp, d), y_s.reshape(bs, ss, d)
```
